```python
import jax, jax.numpy as jnp
from jax import lax
import numpy as np

D_MODEL = 1024
BATCH = 32
SEQ = 256
DEPTH = 4
DEC_BATCH = 8
DEC_SEQ = 2048
PAST_LEN = 512

GRID_W = 64
H_H = 8
HK = 128
HV = 128
H_G = 8
GK = 128
GV = 128
D_FF = 2816
QKV_CONV = 5
FFN_CONV = 3
HGRN_CHUNK = 32
GDN_CHUNK = 64
N_DIR = 2
N_MOD = 6
EPS = 1e-6
SPLIT_SIZES = (H_H * HK, N_DIR * H_H * HK, H_H * HV, H_H * HV,
               2 * H_G * GK + H_G * GV, H_G * GV, N_DIR * H_G, N_DIR * H_G, 2 * D_MODEL)
D_IN = sum(SPLIT_SIZES)

kernel_name = 'bidir_hgrn2_gdn_convffn_diffusion_step'


def rmsnorm(x, g):
    xf = x.astype(jnp.float32)
    y = xf * lax.rsqrt(jnp.mean(xf * xf, axis=-1, keepdims=True) + EPS)
    return (y * g.astype(jnp.float32)).astype(x.dtype)


def l2norm(x):
    xf = x.astype(jnp.float32)
    return xf * lax.rsqrt(jnp.sum(xf * xf, axis=-1, keepdims=True) + EPS)


def _split_cols(z):
    idx = [int(i) for i in np.cumsum(SPLIT_SIZES)[:-1]]
    return jnp.split(z, idx, axis=-1)


def short_conv(x, w):
    pad = QKV_CONV // 2
    seq = x.shape[1]
    xp = jnp.pad(x, ((0, 0), (pad, pad), (0, 0)))
    return sum(w[j] * xp[:, j:j + seq] for j in range(QKV_CONV))


def grid_dwconv(u, w, b, rows, cols):
    bsz, seq, ch = u.shape
    img = u.reshape(bsz, rows, cols, ch)
    out = lax.conv_general_dilated(img, w[:, :, None, :].astype(u.dtype), (1, 1), 'SAME',
                                   dimension_numbers=('NHWC', 'HWIO', 'NHWC'),
                                   feature_group_count=ch)
    return out.reshape(bsz, seq, ch) + b


def conv_ffn(h, w_in, cw, cb, w_out, rows, cols):
    a, b = jnp.split(h @ w_in, 2, axis=-1)
    a = grid_dwconv(a, cw, cb, rows, cols)
    return (jax.nn.silu(a) * b) @ w_out


def _bidir(t):
    return jnp.stack([t, jnp.flip(t, axis=1)])


def _dirs(t):
    t = jnp.moveaxis(t, 2, 0)
    return jnp.stack([t[0], jnp.flip(t[1], axis=1)])


def _merge_dirs(o):
    return o[0] + jnp.flip(o[1], axis=1)


def _hgrn2_chunk_scan(q, k, v, logf, s0):
    f32 = jnp.float32
    nd, bsz, seq, nh, dk = q.shape
    dv = v.shape[-1]
    c = HGRN_CHUNK
    n = seq // c

    def to_chunks(t):
        return jnp.moveaxis(t.astype(f32).reshape(nd, bsz, n, c, nh, t.shape[-1]), 2, 0)

    causal = jnp.tril(jnp.ones((c, c), dtype=bool))[:, :, None, None]

    def step(state, inp):
        qc, kc, vc, gc = inp
        b = jnp.cumsum(gc, axis=2)
        diff = b[:, :, :, None] - b[:, :, None, :]
        decay = jnp.exp(jnp.where(causal, diff, -jnp.inf))
        scores = jnp.einsum('zbthk,zbshk,zbtshk->zbhts', qc, kc, decay)
        o = (jnp.einsum('zbhts,zbshv->zbthv', scores, vc)
             + jnp.einsum('zbthk,zbhkv->zbthv', qc * jnp.exp(b), state))
        b_last = b[:, :, -1]
        state = (jnp.exp(b_last)[..., None] * state
                 + jnp.einsum('zbshk,zbshv->zbhkv', kc * jnp.exp(b_last[:, :, None] - b), vc))
        return state, o

    s_fin, o = lax.scan(step, s0.astype(f32), tuple(to_chunks(t) for t in (q, k, v, logf)))
    o = jnp.moveaxis(o, 0, 2).reshape(nd, bsz, seq, nh, dv)
    return o, s_fin


def _gdn_chunk_scan(q, k, v, g, beta, s0):
    f32 = jnp.float32
    nd, bsz, seq, nh, dk = q.shape
    dv = v.shape[-1]
    c = GDN_CHUNK
    n = seq // c

    def to_chunks(t):
        t = t.astype(f32).reshape((nd, bsz, n, c, nh) + t.shape[4:])
        return jnp.swapaxes(t, 3, 4)

    q, k, v, g, beta = map(to_chunks, (q, k, v, g, beta))
    G = jnp.cumsum(g, axis=-1)
    incl = jnp.tril(jnp.ones((c, c), dtype=bool))
    strict = jnp.tril(jnp.ones((c, c), dtype=bool), -1)
    decay = jnp.exp(jnp.where(incl, G[..., :, None] - G[..., None, :], -jnp.inf))
    kb = k * beta[..., None]
    a_mat = jnp.where(strict, jnp.einsum('...tk,...sk->...ts', kb, k) * decay, 0.0)
    rhs = jnp.concatenate([v * beta[..., None], kb * jnp.exp(G)[..., None]], axis=-1)
    sol = lax.linalg.triangular_solve(a_mat, rhs, left_side=True, lower=True, unit_diagonal=True)
    u, w = sol[..., :dv], sol[..., dv:]
    qk = jnp.where(incl, jnp.einsum('...tk,...sk->...ts', q, k) * decay, 0.0)
    qd = q * jnp.exp(G)[..., None]
    kd = k * jnp.exp(G[..., -1:] - G)[..., None]
    gl = jnp.exp(G[..., -1])

    def step(state, inp):
        u_c, w_c, qk_c, qd_c, kd_c, gl_c = inp
        v_new = u_c - jnp.einsum('zbhck,zbhkv->zbhcv', w_c, state)
        o = (jnp.einsum('zbhck,zbhkv->zbhcv', qd_c, state)
             + jnp.einsum('zbhts,zbhsv->zbhtv', qk_c, v_new))
        state = gl_c[..., None, None] * state + jnp.einsum('zbhck,zbhcv->zbhkv', kd_c, v_new)
        return state, o

    xs = tuple(jnp.moveaxis(t, 2, 0) for t in (u, w, qk, qd, kd, gl))
    s_fin, o = lax.scan(step, s0.astype(f32), xs)
    o = jnp.swapaxes(jnp.moveaxis(o, 0, 2), 3, 4).reshape(nd, bsz, seq, nh, dv)
    return o, s_fin


def token_mixer(h, w_in, lb, gdn_conv, a_log, dt_bias, hgrn_norm, gdn_norm,
                w_proj_h, w_proj_g, w_out, s_h0, s_g0):
    f32 = jnp.float32
    bsz, seq, _ = h.shape
    z = h @ w_in
    q_h, f_h, i_h, og_h, qkv_g, og_g, b_g, a_g, gates = _split_cols(z)

    q = jax.nn.silu(q_h.astype(f32)).reshape(bsz, seq, H_H, HK) * HK ** -0.5
    f_logit = f_h.astype(f32).reshape(bsz, seq, N_DIR, H_H * HK)
    logf = jnp.logaddexp(jnp.log(lb), jnp.log1p(-lb) + jax.nn.log_sigmoid(f_logit))
    logf = _dirs(logf.reshape(bsz, seq, N_DIR, H_H, HK))
    k = -jnp.expm1(logf)
    v = i_h.astype(f32).reshape(bsz, seq, H_H, HV)
    o_h, s_h = _hgrn2_chunk_scan(_bidir(q), k, _bidir(v), logf, s_h0)
    o_h = rmsnorm(_merge_dirs(o_h), hgrn_norm) * jax.nn.silu(og_h.astype(f32)).reshape(bsz, seq, H_H, HV)

    qkv = jax.nn.silu(short_conv(qkv_g, gdn_conv).astype(f32))
    qg, kg, vg = jnp.split(qkv, [H_G * GK, 2 * H_G * GK], axis=-1)
    qg = l2norm(qg.reshape(bsz, seq, H_G, GK)) * GK ** -0.5
    kg = l2norm(kg.reshape(bsz, seq, H_G, GK))
    vg = vg.reshape(bsz, seq, H_G, GV)
    beta = _dirs(jax.nn.sigmoid(b_g.astype(f32).reshape(bsz, seq, N_DIR, H_G)))
    g = _dirs(-jnp.exp(a_log.astype(f32))
              * jax.nn.softplus(a_g.astype(f32).reshape(bsz, seq, N_DIR, H_G) + dt_bias.astype(f32)))
    o_g, s_g = _gdn_chunk_scan(_bidir(qg), _bidir(kg), _bidir(vg), g, beta, s_g0)
    o_g = rmsnorm(_merge_dirs(o_g), gdn_norm) * jax.nn.silu(og_g.astype(f32)).reshape(bsz, seq, H_G, GV)

    gate_h, gate_g = jnp.split(jax.nn.sigmoid(gates), 2, axis=-1)
    y = (gate_h * (o_h.reshape(bsz, seq, H_H * HV).astype(h.dtype) @ w_proj_h)
         + gate_g * (o_g.reshape(bsz, seq, H_G * GV).astype(h.dtype) @ w_proj_g))
    return y @ w_out, s_h, s_g


def setup_inputs(seed: int = 0) -> dict:
    key = jax.random.key(seed)
    ks = jax.random.split(key, 32)
    f32 = jnp.float32

    def nrm(k, shape, s):
        return jax.random.normal(k, shape, f32) * s

    def gain(k, shape):
        return 1.0 + nrm(k, shape, 0.01)

    dt = jnp.exp(jax.random.uniform(ks[15], (DEPTH, N_DIR, H_G), f32, np.log(1e-3), np.log(1e-1)))
    return {
        'x_prompt': nrm(ks[0], (BATCH, SEQ, D_MODEL), 1.0),
        'x_sample': nrm(ks[1], (DEC_BATCH, DEC_SEQ, D_MODEL), 1.0),
        'c': nrm(ks[2], (DEC_BATCH, D_MODEL), 1.0),
        'state_hgrn': nrm(ks[3], (DEC_BATCH, DEPTH, N_DIR, H_H, HK, HV), 0.5),
        'state_gdn': nrm(ks[4], (DEC_BATCH, DEPTH, N_DIR, H_G, GK, GV), 0.1),
        'c_ctx': nrm(ks[5], (D_MODEL,), 1.0),
        'w_ada': nrm(ks[6], (DEPTH, D_MODEL, N_MOD * D_MODEL), 0.5 * D_MODEL ** -0.5),
        'b_ada': nrm(ks[7], (DEPTH, N_MOD * D_MODEL), 0.01),
        'norm_mix': gain(ks[8], (DEPTH, D_MODEL)),
        'norm_ffn': gain(ks[9], (DEPTH, D_MODEL)),
        'w_in': nrm(ks[10], (DEPTH, D_MODEL, D_IN), D_MODEL ** -0.5),
        'hgrn_lb': nrm(ks[11], (DEPTH, N_DIR, H_H * HK), 0.1),
        'hgrn_norm': gain(ks[12], (DEPTH, HV)),
        'gdn_conv': nrm(ks[13], (DEPTH, QKV_CONV, 2 * H_G * GK + H_G * GV), QKV_CONV ** -0.5),
        'gdn_a_log': jnp.log(jax.random.uniform(ks[14], (DEPTH, N_DIR, H_G), f32, 1.0, 16.0)),
        'gdn_dt_bias': dt + jnp.log(-jnp.expm1(-dt)),
        'gdn_norm': gain(ks[16], (DEPTH, GV)),
        'w_proj_h': nrm(ks[17], (DEPTH, H_H * HV, D_MODEL), (H_H * HV) ** -0.5),
        'w_proj_g': nrm(ks[18], (DEPTH, H_G * GV, D_MODEL), (H_G * GV) ** -0.5),
        'w_out': nrm(ks[19], (DEPTH, D_MODEL, D_MODEL), D_MODEL ** -0.5),
        'w_ffn_in': nrm(ks[20], (DEPTH, D_MODEL, 2 * D_FF), D_MODEL ** -0.5),
        'ffn_conv': nrm(ks[21], (DEPTH, FFN_CONV, FFN_CONV, D_FF), 1.0 / FFN_CONV),
        'ffn_conv_b': nrm(ks[22], (DEPTH, D_FF), 0.01),
        'w_ffn_out': nrm(ks[23], (DEPTH, D_FF, D_MODEL), D_FF ** -0.5),
        'norm_final': gain(ks[24], (D_MODEL,)),
    }


def reference(x_prompt, x_sample, c, state_hgrn, state_gdn, c_ctx, w_ada, b_ada, norm_mix, norm_ffn,
              w_in, hgrn_lb, hgrn_norm, gdn_conv, gdn_a_log, gdn_dt_bias, gdn_norm, w_proj_h, w_proj_g,
              w_out, w_ffn_in, ffn_conv, ffn_conv_b, w_ffn_out, norm_final):
    f32 = jnp.float32
    lb_cum = jnp.cumsum(jax.nn.softmax(hgrn_lb.astype(f32), axis=0), axis=0)
    lower_bounds = lb_cum - lb_cum[0]

    def run_layer(l, x, cond, rows, cols, s_h0, s_g0):
        ada = jax.nn.silu(cond) @ w_ada[l] + b_ada[l]
        sh1, sc1, g1, sh2, sc2, g2 = [t[:, None] for t in jnp.split(ada, N_MOD, axis=-1)]
        h = rmsnorm(x, norm_mix[l]) * (1 + sc1) + sh1
        mix, s_h, s_g = token_mixer(h, w_in[l], lower_bounds[l], gdn_conv[l], gdn_a_log[l], gdn_dt_bias[l],
                                    hgrn_norm[l], gdn_norm[l], w_proj_h[l], w_proj_g[l], w_out[l], s_h0, s_g0)
        x = x + g1 * mix
        h = rmsnorm(x, norm_ffn[l]) * (1 + sc2) + sh2
        x = x + g2 * conv_ffn(h, w_ffn_in[l], ffn_conv[l], ffn_conv_b[l], w_ffn_out[l], rows, cols)
        return x, s_h, s_g

    ctx_b, ctx_len, _ = x_prompt.shape
    zero_h = jnp.zeros((N_DIR, ctx_b, H_H, HK, HV), f32)
    zero_g = jnp.zeros((N_DIR, ctx_b, H_G, GK, GV), f32)
    xc = x_prompt
    hs, gs = [], []
    for l in range(DEPTH):
        xc, s_h, s_g = run_layer(l, xc, c_ctx[None], 1, ctx_len, zero_h, zero_g)
        hs.append(jnp.moveaxis(s_h, 0, 1))
        gs.append(jnp.moveaxis(s_g, 0, 1))
    y_prompt = rmsnorm(xc, norm_final)
    new_state_hgrn = jnp.stack(hs, axis=1)
    new_state_gdn = jnp.stack(gs, axis=1)

    rows = x_sample.shape[1] // GRID_W
    xs = x_sample
    for l in range(DEPTH):
        xs, _, _ = run_layer(l, xs, c, rows, GRID_W,
                             jnp.moveaxis(state_hgrn[:, l], 1, 0), jnp.moveaxis(state_gdn[:, l], 1, 0))
    y_sample = rmsnorm(xs, norm_final)
    return (y_prompt, y_sample, new_state_hgrn, new_state_gdn)
```

```python
import functools
import math

import jax
import jax.numpy as jnp
import numpy as np
from jax import lax
from jax.experimental import pallas as pl
from jax.experimental.pallas import tpu as pltpu

F32 = jnp.float32
BF16 = jnp.bfloat16

D_MODEL = 1024
DEPTH = 4
GRID_W = 64
N_HEADS = 8
HEAD_DIM = 128
D_FF = 2816
QKV_CONV = 5
N_DIR = 2
N_MOD = 6
EPS = 1e-6

CHUNK = 128
N_LEVELS = int(math.log2(CHUNK))

COL_Q, COL_F, COL_I, COL_OGH, COL_QKV, COL_OGG, COL_GATE_H, COL_GATE_G = 0, 1, 3, 4, 5, 8, 9, 10
COL_TAIL = 11 * D_MODEL
TAIL_W = 256
D_IN_PAD = COL_TAIL + TAIL_W
TAIL_BETA, TAIL_DECAY = 0, N_DIR * N_HEADS

VMEM_LIMIT_BYTES = 60 * 1024 * 1024


def _cparams(sem):
    return pltpu.CompilerParams(dimension_semantics=sem, vmem_limit_bytes=VMEM_LIMIT_BYTES)


def _bdot(a, b):
    return jnp.dot(a.astype(BF16), b.astype(BF16), preferred_element_type=F32)


def _bdot_nt(a, b):
    return lax.dot_general(a.astype(BF16), b.astype(BF16), (((1,), (1,)), ((), ())),
                           preferred_element_type=F32)


def _bdot_tn(a, b):
    return lax.dot_general(a.astype(BF16), b.astype(BF16), (((0,), (0,)), ((), ())),
                           preferred_element_type=F32)


def _hdot(a, b):
    return jnp.dot(a, b, precision=lax.Precision.HIGHEST, preferred_element_type=F32)


def _sigmoid(x):
    return 1.0 / (1.0 + jnp.exp(-x))


def _silu(x):
    return x * _sigmoid(x)


def _ada_kernel(c_ref, w_ref, b_ref, o_ref):
    o_ref[0] = _bdot(_silu(c_ref[...]), w_ref[0]) + b_ref[0]


def _ada_call(cond, w_ada, b_ada):
    rows = cond.shape[0]
    tn = 1536
    return pl.pallas_call(
        _ada_kernel,
        grid=(DEPTH, N_MOD * D_MODEL // tn),
        in_specs=[pl.BlockSpec((rows, D_MODEL), lambda l, j: (0, 0)),
                  pl.BlockSpec((1, D_MODEL, tn), lambda l, j: (l, 0, j)),
                  pl.BlockSpec((1, 1, tn), lambda l, j: (l, 0, j))],
        out_specs=pl.BlockSpec((1, rows, tn), lambda l, j: (l, 0, j)),
        out_shape=jax.ShapeDtypeStruct((DEPTH, rows, N_MOD * D_MODEL), F32),
        compiler_params=_cparams(("parallel", "parallel")),
        name="ada_mod",
    )(cond, w_ada, b_ada.reshape(DEPTH, 1, N_MOD * D_MODEL))


def _norm_mod(x, gain, shift, scale):
    y = x * lax.rsqrt(jnp.mean(x * x, axis=-1, keepdims=True) + EPS)
    return (y * gain) * (1.0 + scale) + shift


def _inproj_kernel(x_ref, sh_ref, sc_ref, g_ref, w_ref, o_ref, h_scr):
    @pl.when(pl.program_id(1) == 0)
    def _():
        h_scr[...] = _norm_mod(x_ref[...], g_ref[...], sh_ref[0], sc_ref[0]).astype(BF16)

    o_ref[...] = jnp.dot(h_scr[...], w_ref[...], preferred_element_type=F32)


def _inproj_call(x, mod, gain, w, rows_per_mod, tm, tn):
    n = x.shape[0]
    mod_idx = lambda i: (i * tm) // rows_per_mod
    return pl.pallas_call(
        _inproj_kernel,
        grid=(n // tm, D_IN_PAD // tn),
        in_specs=[pl.BlockSpec((tm, D_MODEL), lambda i, j: (i, 0)),
                  pl.BlockSpec((1, 1, D_MODEL), lambda i, j: (mod_idx(i), 0, 0)),
                  pl.BlockSpec((1, 1, D_MODEL), lambda i, j: (mod_idx(i), 0, 1)),
                  pl.BlockSpec((1, D_MODEL), lambda i, j: (0, 0)),
                  pl.BlockSpec((D_MODEL, tn), lambda i, j: (0, j))],
        out_specs=pl.BlockSpec((tm, tn), lambda i, j: (i, j)),
        out_shape=jax.ShapeDtypeStruct((n, D_IN_PAD), F32),
        scratch_shapes=[pltpu.VMEM((tm, D_MODEL), BF16)],
        compiler_params=_cparams(("parallel", "arbitrary")),
        name="in_proj",
    )(x, mod, mod, gain, w)


def _qkvconv_kernel(z_ref, w_ref, o_ref, *, seq):
    jc = pl.program_id(1)
    x = z_ref[...]
    t = lax.broadcasted_iota(jnp.int32, (seq, 1), 0)
    pad = QKV_CONV // 2
    acc = x * w_ref[pad:pad + 1, :]
    for s in range(1, pad + 1):
        prev = jnp.where(t >= s, pltpu.roll(x, s, 0), 0.0)
        nxt = jnp.where(t < seq - s, pltpu.roll(x, seq - s, 0), 0.0)
        acc = acc + prev * w_ref[pad - s:pad - s + 1, :] + nxt * w_ref[pad + s:pad + s + 1, :]
    y = _silu(acc)
    tiles_per_part = D_MODEL // 256
    is_v = jc >= 2 * tiles_per_part
    scale = jnp.where(jc < tiles_per_part, HEAD_DIM ** -0.5, 1.0)
    for h in range(256 // HEAD_DIM):
        yh = y[:, h * HEAD_DIM:(h + 1) * HEAD_DIM]
        ss = jnp.sum(yh * yh, axis=-1, keepdims=True)
        normed = yh * (lax.rsqrt(ss + EPS) * scale)
        o_ref[:, h * HEAD_DIM:(h + 1) * HEAD_DIM] = jnp.where(is_v, yh, normed)


def _qkvconv_call(z, conv_w, n_seq, seq):
    tc = 256
    n_tiles = 3 * D_MODEL // tc
    base = COL_QKV * D_MODEL // tc
    return pl.pallas_call(
        functools.partial(_qkvconv_kernel, seq=seq),
        grid=(n_seq, n_tiles),
        in_specs=[pl.BlockSpec((seq, tc), lambda b, j: (b, base + j)),
                  pl.BlockSpec((QKV_CONV, tc), lambda b, j: (0, j))],
        out_specs=pl.BlockSpec((seq, tc), lambda b, j: (b, j)),
        out_shape=jax.ShapeDtypeStruct((n_seq * seq, 3 * D_MODEL), F32),
        compiler_params=_cparams(("parallel", "parallel")),
        name="qkv_conv",
    )(z, conv_w)


def _scan_tables():
    c = CHUNK
    idx = np.arange(c)
    tri = np.zeros((N_DIR, c, c), np.float32)
    tri[0] = (idx[None, :] <= idx[:, None])
    tri[1] = (idx[None, :] >= idx[:, None])
    sel = np.zeros((N_DIR, N_LEVELS, c, c), np.float32)
    pmask = np.zeros((N_DIR, N_LEVELS, c, c), np.float32)
    qrow = np.zeros((N_DIR, N_LEVELS, c, 1), np.float32)
    for lv in range(N_LEVELS):
        m = 1 << lv
        blk = idx // (2 * m)
        upper = (idx // m) % 2 == 1
        same = blk[:, None] == blk[None, :]
        ref_f = blk * 2 * m + m - 1
        sel[0, lv, idx, ref_f] = 1.0
        pmask[0, lv] = same & upper[:, None] & ~upper[None, :]
        qrow[0, lv, :, 0] = upper
        ref_b = blk * 2 * m + m
        sel[1, lv, idx, ref_b] = 1.0
        pmask[1, lv] = same & ~upper[:, None] & upper[None, :]
        qrow[1, lv, :, 0] = ~upper
    return (jnp.asarray(tri), jnp.asarray(sel.reshape(N_DIR, N_LEVELS * c, c)),
            jnp.asarray(pmask), jnp.asarray(qrow))


def _chunk_index(d, j, n_chunks):
    return j + d * (n_chunks - 1 - 2 * j)


def _hgrn_kernel(*refs, has_s0, has_out):
    it = iter(refs)
    q_ref, f_ref, v_ref, lb_ref, tri_ref, sel_ref, pm_ref, qr_ref = (next(it) for _ in range(8))
    s0_ref = next(it) if has_s0 else None
    o_ref = next(it)
    sfin_ref = next(it) if has_out else None
    s_scr = next(it)
    j = pl.program_id(2)
    n_chunks = pl.num_programs(2)
    c = CHUNK

    @pl.when(j == 0)
    def _():
        if has_s0:
            s_scr[...] = s0_ref[0, 0, 0]
        else:
            s_scr[...] = jnp.zeros_like(s_scr)

    x = f_ref[...]
    logsig = jnp.minimum(x, 0.0) - jnp.log1p(jnp.exp(-jnp.abs(x)))
    la = lb_ref[0, 0:1, :]
    lc = lb_ref[0, 1:2, :] + logsig
    logf = jnp.maximum(la, lc) + jnp.log1p(jnp.exp(-jnp.abs(la - lc)))
    kf = jnp.exp(lc - x)
    q = _silu(q_ref[...]) * HEAD_DIM ** -0.5
    v = v_ref[...]
    tri = tri_ref[0]
    b = _hdot(tri, logf)
    btot = jnp.sum(logf, axis=0, keepdims=True)
    ref_all = _hdot(sel_ref[0], b)
    eye = (lax.broadcasted_iota(jnp.int32, (c, c), 0) == lax.broadcasted_iota(jnp.int32, (c, c), 1))

    for h in range(N_HEADS):
        hs = slice(h * HEAD_DIM, (h + 1) * HEAD_DIM)
        qh, kh, vh, bh = q[:, hs], kf[:, hs], v[:, hs], b[:, hs]
        scores = jnp.where(eye, _bdot_nt(qh, kh), 0.0)
        for lv in range(N_LEVELS):
            rl = ref_all[lv * c:(lv + 1) * c, hs]
            is_q = qr_ref[0, lv] > 0.5
            qt = jnp.where(is_q, qh * jnp.exp(jnp.minimum(bh - rl, 0.0)), 0.0)
            kt = jnp.where(is_q, 0.0, kh * jnp.exp(jnp.minimum(rl - bh, 0.0)))
            scores = scores + pm_ref[0, lv] * _bdot_nt(qt, kt)
        s_prev = s_scr[h]
        o_ref[0, :, hs] = _bdot(scores, vh) + _bdot(qh * jnp.exp(bh), s_prev)
        bt = btot[:, hs]
        kdec = kh * jnp.exp(bt - bh)
        e_col = jnp.transpose(jnp.broadcast_to(jnp.exp(bt), (HEAD_DIM, HEAD_DIM)))[:, 0:1]
        s_scr[h] = e_col * s_prev + _bdot_tn(kdec, vh)

    if has_out:
        @pl.when(j == n_chunks - 1)
        def _():
            sfin_ref[0, 0] = s_scr[...]


def _hgrn_call(z, lb_logs, tables, s0, layer, n_seq, seq, has_out):
    n_chunks = seq // CHUNK
    has_s0 = s0 is not None
    tri, sel, pmask, qrow = tables
    row = lambda d, b, j: b * n_chunks + _chunk_index(d, j, n_chunks)
    in_specs = [pl.BlockSpec((CHUNK, D_MODEL), lambda d, b, j: (row(d, b, j), COL_Q)),
                pl.BlockSpec((CHUNK, D_MODEL), lambda d, b, j: (row(d, b, j), COL_F + d)),
                pl.BlockSpec((CHUNK, D_MODEL), lambda d, b, j: (row(d, b, j), COL_I)),
                pl.BlockSpec((1, 2, D_MODEL), lambda d, b, j: (d, 0, 0)),
                pl.BlockSpec((1, CHUNK, CHUNK), lambda d, b, j: (d, 0, 0)),
                pl.BlockSpec((1, N_LEVELS * CHUNK, CHUNK), lambda d, b, j: (d, 0, 0)),
                pl.BlockSpec((1, N_LEVELS, CHUNK, CHUNK), lambda d, b, j: (d, 0, 0, 0)),
                pl.BlockSpec((1, N_LEVELS, CHUNK, 1), lambda d, b, j: (d, 0, 0, 0))]
    args = [z, z, z, lb_logs, tri, sel, pmask, qrow]
    if has_s0:
        in_specs.append(pl.BlockSpec((1, 1, 1, N_HEADS, HEAD_DIM, HEAD_DIM),
                                     lambda d, b, j: (b, layer, d, 0, 0, 0)))
        args.append(s0)
    out_specs = [pl.BlockSpec((1, CHUNK, D_MODEL), lambda d, b, j: (d, row(d, b, j), 0))]
    out_shape = [jax.ShapeDtypeStruct((N_DIR, n_seq * seq, D_MODEL), F32)]
    if has_out:
        out_specs.append(pl.BlockSpec((1, 1, N_HEADS, HEAD_DIM, HEAD_DIM), lambda d, b, j: (b, d, 0, 0, 0)))
        out_shape.append(jax.ShapeDtypeStruct((n_seq, N_DIR, N_HEADS, HEAD_DIM, HEAD_DIM), F32))
    res = pl.pallas_call(
        functools.partial(_hgrn_kernel, has_s0=has_s0, has_out=has_out),
        grid=(N_DIR, n_seq, n_chunks),
        in_specs=in_specs, out_specs=out_specs, out_shape=out_shape,
        scratch_shapes=[pltpu.VMEM((N_HEADS, HEAD_DIM, HEAD_DIM), F32)],
        compiler_params=_cparams(("parallel", "parallel", "arbitrary")),
        name="hgrn_scan",
    )(*args)
    return res if has_out else (res[0], None)


def _triangular_inverse(a, eye_f, pm_ref):
    t_inv = eye_f - pm_ref[0, 0] * a
    for lv in range(1, N_LEVELS):
        t_inv = t_inv - _hdot(_hdot(t_inv, pm_ref[0, lv] * a), t_inv)
    return t_inv


def _gdn_kernel(*refs, has_s0, has_out):
    it = iter(refs)
    qkv_refs = [next(it) for _ in range(3)]
    tail_ref, par_ref, tri_ref, pm_ref = next(it), next(it), next(it), next(it)
    s0_ref = next(it) if has_s0 else None
    o_ref = next(it)
    sfin_ref = next(it) if has_out else None
    s_scr = next(it)
    d = pl.program_id(0)
    j = pl.program_id(2)
    n_chunks = pl.num_programs(2)
    c = CHUNK

    @pl.when(j == 0)
    def _():
        if has_s0:
            s_scr[...] = s0_ref[0, 0, 0]
        else:
            s_scr[...] = jnp.zeros_like(s_scr)

    tail = tail_ref[...]
    beta_all = _sigmoid(tail)
    xg = tail + par_ref[1:2, :]
    softplus = jnp.maximum(xg, 0.0) + jnp.log1p(jnp.exp(-jnp.abs(xg)))
    g_all = par_ref[0:1, :] * softplus
    tri = tri_ref[0]
    gcum = _hdot(tri, g_all)
    gcum_t = jnp.transpose(gcum)
    gtot = jnp.sum(g_all, axis=0, keepdims=True)
    incl = tri > 0.5
    row_i = lax.broadcasted_iota(jnp.int32, (c, c), 0)
    col_i = lax.broadcasted_iota(jnp.int32, (c, c), 1)
    eye = row_i == col_i
    strict = jnp.logical_and(incl, jnp.logical_not(eye))
    eye_f = eye.astype(F32)
    q_all, k_all, v_all = (r[...] for r in qkv_refs)

    def pick(arr_fn, base, h):
        return jnp.where(d == 0, arr_fn(base + h), arr_fn(base + N_HEADS + h))

    for h in range(N_HEADS):
        hs = slice(h * HEAD_DIM, (h + 1) * HEAD_DIM)
        qh, kh, vh = q_all[:, hs], k_all[:, hs], v_all[:, hs]
        beta = pick(lambda i: beta_all[:, i:i + 1], TAIL_BETA, h)
        g_c = pick(lambda i: gcum[:, i:i + 1], TAIL_DECAY, h)
        g_r = pick(lambda i: gcum_t[i:i + 1, :], TAIL_DECAY, h)
        g_t = pick(lambda i: gtot[:, i:i + 1], TAIL_DECAY, h)
        decay = jnp.where(incl, jnp.exp(jnp.where(incl, g_c - g_r, 0.0)), 0.0)
        a_mat = jnp.where(strict, beta * _bdot_nt(kh, kh) * decay, 0.0)
        t_inv = _triangular_inverse(a_mat, eye_f, pm_ref)
        kb = kh * beta
        u = _hdot(t_inv, vh * beta)
        w = _hdot(t_inv, kb * jnp.exp(g_c))
        qk = jnp.where(incl, _bdot_nt(qh, kh) * decay, 0.0)
        s_prev = s_scr[h]
        v_new = u - _bdot(w, s_prev)
        o_ref[0, :, hs] = _bdot(qh * jnp.exp(g_c), s_prev) + _bdot(qk, v_new)
        kd = kh * jnp.exp(g_t - g_c)
        s_scr[h] = jnp.exp(g_t) * s_prev + _bdot_tn(kd, v_new)

    if has_out:
        @pl.when(j == n_chunks - 1)
        def _():
            sfin_ref[0, 0] = s_scr[...]


def _gdn_call(qkv, z, gdn_par, tri, pmask, s0, layer, n_seq, seq, has_out):
    n_chunks = seq // CHUNK
    has_s0 = s0 is not None
    row = lambda d, b, j: b * n_chunks + _chunk_index(d, j, n_chunks)
    in_specs = [pl.BlockSpec((CHUNK, D_MODEL), lambda d, b, j, p=p: (row(d, b, j), p)) for p in range(3)]
    in_specs += [pl.BlockSpec((CHUNK, 128), lambda d, b, j: (row(d, b, j), COL_TAIL // 128)),
                 pl.BlockSpec((2, 128), lambda d, b, j: (0, 0)),
                 pl.BlockSpec((1, CHUNK, CHUNK), lambda d, b, j: (d, 0, 0)),
                 pl.BlockSpec((1, N_LEVELS, CHUNK, CHUNK), lambda d, b, j: (d, 0, 0, 0))]
    args = [qkv, qkv, qkv, z, gdn_par, tri, pmask]
    if has_s0:
        in_specs.append(pl.BlockSpec((1, 1, 1, N_HEADS, HEAD_DIM, HEAD_DIM),
                                     lambda d, b, j: (b, layer, d, 0, 0, 0)))
        args.append(s0)
    out_specs = [pl.BlockSpec((1, CHUNK, D_MODEL), lambda d, b, j: (d, row(d, b, j), 0))]
    out_shape = [jax.ShapeDtypeStruct((N_DIR, n_seq * seq, D_MODEL), F32)]
    if has_out:
        out_specs.append(pl.BlockSpec((1, 1, N_HEADS, HEAD_DIM, HEAD_DIM), lambda d, b, j: (b, d, 0, 0, 0)))
        out_shape.append(jax.ShapeDtypeStruct((n_seq, N_DIR, N_HEADS, HEAD_DIM, HEAD_DIM), F32))
    res = pl.pallas_call(
        functools.partial(_gdn_kernel, has_s0=has_s0, has_out=has_out),
        grid=(N_DIR, n_seq, n_chunks),
        in_specs=in_specs, out_specs=out_specs, out_shape=out_shape,
        scratch_shapes=[pltpu.VMEM((N_HEADS, HEAD_DIM, HEAD_DIM), F32)],
        compiler_params=_cparams(("parallel", "parallel", "arbitrary")),
        name="gdn_scan",
    )(*args)
    return res if has_out else (res[0], None)


def _head_rmsnorm(x, gain):
    parts = []
    for h in range(N_HEADS):
        xh = x[:, h * HEAD_DIM:(h + 1) * HEAD_DIM]
        parts.append(xh * lax.rsqrt(jnp.mean(xh * xh, axis=-1, keepdims=True) + EPS))
    return jnp.concatenate(parts, axis=-1) * gain


def _post_kernel(oh_ref, og_ref, zoh_ref, zog_ref, zgh_ref, zgg_ref, x_ref, g1_ref, hn_ref, gn_ref,
                 wph_ref, wpg_ref, wo_ref, o_ref):
    oh = _head_rmsnorm(oh_ref[0] + oh_ref[1], hn_ref[...]) * _silu(zoh_ref[...])
    og = _head_rmsnorm(og_ref[0] + og_ref[1], gn_ref[...]) * _silu(zog_ref[...])
    y = (_sigmoid(zgh_ref[...]) * _bdot(oh, wph_ref[...])
         + _sigmoid(zgg_ref[...]) * _bdot(og, wpg_ref[...]))
    o_ref[...] = x_ref[...] + g1_ref[0] * _bdot(y, wo_ref[...])


def _post_call(oh, og, z, x, mod, hn, gn, wph, wpg, wo, rows_per_mod, tm):
    n = x.shape[0]
    mod_idx = lambda i: (i * tm) // rows_per_mod
    zspec = lambda col: pl.BlockSpec((tm, D_MODEL), lambda i, col=col: (i, col))
    wspec = pl.BlockSpec((D_MODEL, D_MODEL), lambda i: (0, 0))
    vspec = pl.BlockSpec((1, D_MODEL), lambda i: (0, 0))
    return pl.pallas_call(
        _post_kernel,
        grid=(n // tm,),
        in_specs=[pl.BlockSpec((N_DIR, tm, D_MODEL), lambda i: (0, i, 0)),
                  pl.BlockSpec((N_DIR, tm, D_MODEL), lambda i: (0, i, 0)),
                  zspec(COL_OGH), zspec(COL_OGG), zspec(COL_GATE_H), zspec(COL_GATE_G),
                  pl.BlockSpec((tm, D_MODEL), lambda i: (i, 0)),
                  pl.BlockSpec((1, 1, D_MODEL), lambda i: (mod_idx(i), 0, 2)),
                  vspec, vspec, wspec, wspec, wspec],
        out_specs=pl.BlockSpec((tm, D_MODEL), lambda i: (i, 0)),
        out_shape=jax.ShapeDtypeStruct((n, D_MODEL), F32),
        compiler_params=_cparams(("parallel",)),
        name="mix_out",
    )(oh, og, z, z, z, z, x, mod, hn, gn, wph, wpg, wo)


def _ffn_kernel(x_ref, sh_ref, sc_ref, g2_ref, gain_ref, wa_ref, wb_ref, cw_ref, cb_ref, wo_ref, *rest,
                rows, cols, final_norm):
    if final_norm:
        nf_ref, o_ref, h_scr = rest
    else:
        o_ref, h_scr = rest
    j = pl.program_id(1)
    nj = pl.num_programs(1)
    tr = x_ref.shape[0]

    @pl.when(j == 0)
    def _():
        h_scr[...] = _norm_mod(x_ref[...], gain_ref[...], sh_ref[0], sc_ref[0]).astype(BF16)

    h = h_scr[...]
    a = jnp.dot(h, wa_ref[...], preferred_element_type=F32)
    bgate = jnp.dot(h, wb_ref[...], preferred_element_type=F32)
    t = lax.broadcasted_iota(jnp.int32, (tr, 1), 0)
    col = t & (cols - 1)
    a_l = jnp.where(col >= 1, pltpu.roll(a, 1, 0), 0.0)
    a_r = jnp.where(col <= cols - 2, pltpu.roll(a, tr - 1, 0), 0.0)

    def hrow(dr):
        return a_l * cw_ref[dr, 0:1, :] + a * cw_ref[dr, 1:2, :] + a_r * cw_ref[dr, 2:3, :]

    conv = hrow(1)
    if rows > 1:
        r = (t >> int(math.log2(cols))) & (rows - 1)
        conv = conv + jnp.where(r >= 1, pltpu.roll(hrow(0), cols, 0), 0.0)
        conv = conv + jnp.where(r <= rows - 2, pltpu.roll(hrow(2), tr - cols, 0), 0.0)
    conv = conv + cb_ref[...]
    part = _bdot(_silu(conv) * bgate, wo_ref[...])

    @pl.when(j == 0)
    def _():
        o_ref[...] = part

    @pl.when(j > 0)
    def _():
        o_ref[...] += part

    @pl.when(j == nj - 1)
    def _():
        y = x_ref[...] + g2_ref[0] * o_ref[...]
        if final_norm:
            y = y * lax.rsqrt(jnp.mean(y * y, axis=-1, keepdims=True) + EPS) * nf_ref[...]
        o_ref[...] = y


def _ffn_call(x, mod, gain, w_in, cw, cb, w_out, rows_per_mod, tr, rows, cols, norm_final):
    n = x.shape[0]
    tf = 256
    nj = D_FF // tf
    mod_idx = lambda i: (i * tr) // rows_per_mod
    mspec = lambda k: pl.BlockSpec((1, 1, D_MODEL), lambda i, j, k=k: (mod_idx(i), 0, k))
    in_specs = [pl.BlockSpec((tr, D_MODEL), lambda i, j: (i, 0)),
                mspec(3), mspec(4), mspec(5),
                pl.BlockSpec((1, D_MODEL), lambda i, j: (0, 0)),
                pl.BlockSpec((D_MODEL, tf), lambda i, j: (0, j)),
                pl.BlockSpec((D_MODEL, tf), lambda i, j: (0, nj + j)),
                pl.BlockSpec((3, 3, tf), lambda i, j: (0, 0, j)),
                pl.BlockSpec((1, tf), lambda i, j: (0, j)),
                pl.BlockSpec((tf, D_MODEL), lambda i, j: (j, 0))]
    args = [x, mod, mod, mod, gain, w_in, w_in, cw, cb, w_out]
    final_norm = norm_final is not None
    if final_norm:
        in_specs.append(pl.BlockSpec((1, D_MODEL), lambda i, j: (0, 0)))
        args.append(norm_final)
    return pl.pallas_call(
        functools.partial(_ffn_kernel, rows=rows, cols=cols, final_norm=final_norm),
        grid=(n // tr, nj),
        in_specs=in_specs,
        out_specs=pl.BlockSpec((tr, D_MODEL), lambda i, j: (i, 0)),
        out_shape=jax.ShapeDtypeStruct((n, D_MODEL), F32),
        scratch_shapes=[pltpu.VMEM((tr, D_MODEL), BF16)],
        compiler_params=_cparams(("parallel", "arbitrary")),
        name="conv_ffn",
    )(*args)


def _run_pass(x, mod_of_layer, rows_per_mod, n_seq, seq, conv_rows, conv_cols, s_h0, s_g0, want_states,
              prm, tables):
    tri = tables[0]
    n = n_seq * seq
    assert conv_rows & (conv_rows - 1) == 0 and conv_cols & (conv_cols - 1) == 0
    ffn_rows = min(rows_per_mod, max(seq, 2048))
    assert ffn_rows % seq == 0 and n % ffn_rows == 0
    hs, gs = [], []
    for l in range(DEPTH):
        mod = mod_of_layer(l)
        z = _inproj_call(x, mod, prm["norm_mix"][l], prm["w_in"][l], rows_per_mod, tm=min(rows_per_mod, 1024), tn=1280)
        oh, s_h = _hgrn_call(z, prm["lb_logs"][l], tables, s_h0, l, n_seq, seq, want_states)
        qkv = _qkvconv_call(z, prm["gdn_conv"][l], n_seq, seq)
        og, s_g = _gdn_call(qkv, z, prm["gdn_par"][l], tri, tables[2], s_g0, l, n_seq, seq, want_states)
        x = _post_call(oh, og, z, x, mod, prm["hgrn_norm"][l], prm["gdn_norm"][l], prm["w_proj_h"][l],
                       prm["w_proj_g"][l], prm["w_out"][l], rows_per_mod, tm=256)
        x = _ffn_call(x, mod, prm["norm_ffn"][l], prm["w_ffn_in"][l], prm["ffn_conv"][l], prm["ffn_conv_b"][l],
                      prm["w_ffn_out"][l], rows_per_mod, ffn_rows, conv_rows, conv_cols,
                      prm["norm_final"] if l == DEPTH - 1 else None)
        hs.append(s_h)
        gs.append(s_g)
    return x, hs, gs


def kernel(x_prompt, x_sample, c, state_hgrn, state_gdn, c_ctx, w_ada, b_ada, norm_mix, norm_ffn, w_in, hgrn_lb, hgrn_norm, gdn_conv, gdn_a_log, gdn_dt_bias, gdn_norm, w_proj_h, w_proj_g, w_out, w_ffn_in, ffn_conv, ffn_conv_b, w_ffn_out, norm_final):
    ctx_b, ctx_len, _ = x_prompt.shape
    dec_b, dec_len, _ = x_sample.shape

    lb_cum = jnp.cumsum(jax.nn.softmax(hgrn_lb.astype(F32), axis=0), axis=0)
    lower = lb_cum - lb_cum[0]
    lb_logs = jnp.stack([jnp.log(lower), jnp.log1p(-lower)], axis=2)
    split = COL_GATE_H * D_MODEL
    small = 2 * N_DIR * N_HEADS
    w_in_r = jnp.concatenate(
        [w_in[..., :split], w_in[..., split + small:], w_in[..., split:split + small],
         jnp.zeros((DEPTH, D_MODEL, TAIL_W - small), w_in.dtype)], axis=-1).astype(BF16)
    gdn_par = jnp.zeros((DEPTH, 2, 128), F32)
    gdn_par = gdn_par.at[:, 0, TAIL_DECAY:TAIL_DECAY + N_DIR * N_HEADS].set(
        -jnp.exp(gdn_a_log.astype(F32)).reshape(DEPTH, -1))
    gdn_par = gdn_par.at[:, 1, TAIL_DECAY:TAIL_DECAY + N_DIR * N_HEADS].set(
        gdn_dt_bias.astype(F32).reshape(DEPTH, -1))
    prm = dict(
        norm_mix=norm_mix.reshape(DEPTH, 1, D_MODEL), norm_ffn=norm_ffn.reshape(DEPTH, 1, D_MODEL),
        w_in=w_in_r, lb_logs=lb_logs, gdn_conv=gdn_conv, gdn_par=gdn_par,
        hgrn_norm=jnp.tile(hgrn_norm, (1, N_HEADS)).reshape(DEPTH, 1, D_MODEL),
        gdn_norm=jnp.tile(gdn_norm, (1, N_HEADS)).reshape(DEPTH, 1, D_MODEL),
        w_proj_h=w_proj_h.astype(BF16), w_proj_g=w_proj_g.astype(BF16), w_out=w_out.astype(BF16),
        w_ffn_in=w_ffn_in.astype(BF16), ffn_conv=ffn_conv, ffn_conv_b=ffn_conv_b.reshape(DEPTH, 1, D_FF),
        w_ffn_out=w_ffn_out.astype(BF16), norm_final=norm_final.reshape(1, D_MODEL))
    tables = _scan_tables()

    cond = jnp.concatenate([c_ctx[None], c, jnp.zeros((16 - 1 - dec_b, D_MODEL), F32)], axis=0)
    mod = _ada_call(cond, w_ada, b_ada)

    n_ctx = ctx_b * ctx_len
    yc, hs, gs = _run_pass(x_prompt.reshape(n_ctx, D_MODEL), lambda l: mod[l, 0:1, None, :], n_ctx,
                           ctx_b, ctx_len, 1, ctx_len, None, None, True, prm, tables)
    n_dec = dec_b * dec_len
    yd, _, _ = _run_pass(x_sample.reshape(n_dec, D_MODEL), lambda l: mod[l, 1:1 + dec_b, None, :], dec_len,
                         dec_b, dec_len, dec_len // GRID_W, GRID_W, state_hgrn, state_gdn, False, prm, tables)
    return (yc.reshape(ctx_b, ctx_len, D_MODEL), yd.reshape(dec_b, dec_len, D_MODEL),
            jnp.stack(hs, axis=1), jnp.stack(gs, axis=1))
```

```python
import functools
import math

import jax
import jax.numpy as jnp
import numpy as np
from jax import lax
from jax.experimental import pallas as pl
from jax.experimental.pallas import tpu as pltpu

F32 = jnp.float32
BF16 = jnp.bfloat16

D_MODEL = 1024
DEPTH = 4
GRID_W = 64
N_HEADS = 8
HEAD_DIM = 128
D_FF = 2816
QKV_CONV = 5
N_DIR = 2
N_MOD = 6
EPS = 1e-6

CHUNK = 128
N_LEVELS = int(math.log2(CHUNK))

COL_Q, COL_F, COL_I, COL_OGH, COL_QKV, COL_OGG, COL_GATE_H, COL_GATE_G = 0, 1, 3, 4, 5, 8, 9, 10
COL_TAIL = 11 * D_MODEL
TAIL_W = 256
D_IN_PAD = COL_TAIL + TAIL_W
TAIL_BETA, TAIL_DECAY = 0, N_DIR * N_HEADS

VMEM_LIMIT_BYTES = 60 * 1024 * 1024


def _cparams(sem):
    return pltpu.CompilerParams(dimension_semantics=sem, vmem_limit_bytes=VMEM_LIMIT_BYTES)


def _bdot(a, b):
    return jnp.dot(a.astype(BF16), b.astype(BF16), preferred_element_type=F32)


def _bdot_nt(a, b):
    return lax.dot_general(a.astype(BF16), b.astype(BF16), (((1,), (1,)), ((), ())),
                           preferred_element_type=F32)


def _bdot_tn(a, b):
    return lax.dot_general(a.astype(BF16), b.astype(BF16), (((0,), (0,)), ((), ())),
                           preferred_element_type=F32)


def _sigmoid(x):
    return 1.0 / (1.0 + jnp.exp(-x))


def _silu(x):
    return x * _sigmoid(x)


def _ada_kernel(c_ref, w_ref, b_ref, o_ref):
    o_ref[0] = _bdot(_silu(c_ref[...]), w_ref[0]) + b_ref[0]


def _ada_call(cond, w_ada, b_ada):
    rows = cond.shape[0]
    tn = 1536
    return pl.pallas_call(
        _ada_kernel,
        grid=(DEPTH, N_MOD * D_MODEL // tn),
        in_specs=[pl.BlockSpec((rows, D_MODEL), lambda l, j: (0, 0)),
                  pl.BlockSpec((1, D_MODEL, tn), lambda l, j: (l, 0, j)),
                  pl.BlockSpec((1, 1, tn), lambda l, j: (l, 0, j))],
        out_specs=pl.BlockSpec((1, rows, tn), lambda l, j: (l, 0, j)),
        out_shape=jax.ShapeDtypeStruct((DEPTH, rows, N_MOD * D_MODEL), F32),
        compiler_params=_cparams(("parallel", "parallel")),
        name="ada_mod",
    )(cond, w_ada, b_ada.reshape(DEPTH, 1, N_MOD * D_MODEL))


def _norm_mod(x, gain, shift, scale):
    y = x * lax.rsqrt(jnp.mean(x * x, axis=-1, keepdims=True) + EPS)
    return (y * gain) * (1.0 + scale) + shift


def _inproj_kernel(x_ref, sh_ref, sc_ref, g_ref, w_ref, o_ref, h_scr):
    @pl.when(pl.program_id(1) == 0)
    def _():
        h_scr[...] = _norm_mod(x_ref[...], g_ref[...], sh_ref[0], sc_ref[0]).astype(BF16)

    o_ref[...] = jnp.dot(h_scr[...], w_ref[...], preferred_element_type=F32)


def _inproj_call(x, mod, gain, w, rows_per_mod, tm, tn):
    n = x.shape[0]
    mod_idx = lambda i: (i * tm) // rows_per_mod
    return pl.pallas_call(
        _inproj_kernel,
        grid=(n // tm, D_IN_PAD // tn),
        in_specs=[pl.BlockSpec((tm, D_MODEL), lambda i, j: (i, 0)),
                  pl.BlockSpec((1, 1, D_MODEL), lambda i, j: (mod_idx(i), 0, 0)),
                  pl.BlockSpec((1, 1, D_MODEL), lambda i, j: (mod_idx(i), 0, 1)),
                  pl.BlockSpec((1, D_MODEL), lambda i, j: (0, 0)),
                  pl.BlockSpec((D_MODEL, tn), lambda i, j: (0, j))],
        out_specs=pl.BlockSpec((tm, tn), lambda i, j: (i, j)),
        out_shape=jax.ShapeDtypeStruct((n, D_IN_PAD), F32),
        scratch_shapes=[pltpu.VMEM((tm, D_MODEL), BF16)],
        compiler_params=_cparams(("parallel", "arbitrary")),
        name="in_proj",
    )(x, mod, mod, gain, w)


def _qkvconv_kernel(z_ref, w_ref, o_ref, *, seq):
    jc = pl.program_id(1)
    x = z_ref[...]
    t = lax.broadcasted_iota(jnp.int32, (seq, 1), 0)
    pad = QKV_CONV // 2
    acc = x * w_ref[pad:pad + 1, :]
    for s in range(1, pad + 1):
        prev = jnp.where(t >= s, pltpu.roll(x, s, 0), 0.0)
        nxt = jnp.where(t < seq - s, pltpu.roll(x, seq - s, 0), 0.0)
        acc = acc + prev * w_ref[pad - s:pad - s + 1, :] + nxt * w_ref[pad + s:pad + s + 1, :]
    y = _silu(acc)
    tiles_per_part = D_MODEL // 256
    is_v = jc >= 2 * tiles_per_part
    scale = jnp.where(jc < tiles_per_part, HEAD_DIM ** -0.5, 1.0)
    for h in range(256 // HEAD_DIM):
        yh = y[:, h * HEAD_DIM:(h + 1) * HEAD_DIM]
        ss = jnp.sum(yh * yh, axis=-1, keepdims=True)
        normed = yh * (lax.rsqrt(ss + EPS) * scale)
        o_ref[:, h * HEAD_DIM:(h + 1) * HEAD_DIM] = jnp.where(is_v, yh, normed)


def _qkvconv_call(z, conv_w, n_seq, seq):
    tc = 256
    n_tiles = 3 * D_MODEL // tc
    base = COL_QKV * D_MODEL // tc
    return pl.pallas_call(
        functools.partial(_qkvconv_kernel, seq=seq),
        grid=(n_seq, n_tiles),
        in_specs=[pl.BlockSpec((seq, tc), lambda b, j: (b, base + j)),
                  pl.BlockSpec((QKV_CONV, tc), lambda b, j: (0, j))],
        out_specs=pl.BlockSpec((seq, tc), lambda b, j: (b, j)),
        out_shape=jax.ShapeDtypeStruct((n_seq * seq, 3 * D_MODEL), F32),
        compiler_params=_cparams(("parallel", "parallel")),
        name="qkv_conv",
    )(z, conv_w)


def _scan_tables():
    c = CHUNK
    idx = np.arange(c)
    tri = np.zeros((N_DIR, c, c), np.float32)
    tri[0] = (idx[None, :] <= idx[:, None])
    tri[1] = (idx[None, :] >= idx[:, None])
    cum = np.zeros((N_DIR, N_LEVELS + 2, c, c), np.float32)
    pmask = np.zeros((N_DIR, N_LEVELS, c, c), np.float32)
    qrow = np.zeros((N_DIR, N_LEVELS, c, 1), np.float32)
    cum[:, 0] = tri
    cum[:, 1] = 1.0 - tri
    for lv in range(N_LEVELS):
        m = 1 << lv
        blk = idx // (2 * m)
        upper = (idx // m) % 2 == 1
        same = blk[:, None] == blk[None, :]
        cum[0, 2 + lv] = tri[0] - tri[0][blk * 2 * m + m - 1]
        pmask[0, lv] = same & upper[:, None] & ~upper[None, :]
        qrow[0, lv, :, 0] = upper
        cum[1, 2 + lv] = tri[1] - tri[1][blk * 2 * m + m]
        pmask[1, lv] = same & ~upper[:, None] & upper[None, :]
        qrow[1, lv, :, 0] = ~upper
    cum = cum.reshape(N_DIR, (N_LEVELS + 2) * c, c)
    cum3 = np.concatenate([cum, cum, cum], axis=-1)
    return (jnp.asarray(tri), jnp.asarray(cum3, dtype=BF16), jnp.asarray(pmask), jnp.asarray(qrow))


def _split3(x):
    hi = x.astype(BF16)
    r1 = x - hi.astype(F32)
    mid = r1.astype(BF16)
    lo = (r1 - mid.astype(F32)).astype(BF16)
    return hi, mid, lo


def _dot_table(table3, x):
    return jnp.dot(table3, jnp.concatenate(_split3(x), axis=0), preferred_element_type=F32)


def _dot_x3(a, b):
    ah = a.astype(BF16)
    al = (a - ah.astype(F32)).astype(BF16)
    bh = b.astype(BF16)
    bl = (b - bh.astype(F32)).astype(BF16)
    return jnp.dot(jnp.concatenate([ah, ah, al], axis=1), jnp.concatenate([bh, bl, bh], axis=0),
                   preferred_element_type=F32)


def _chunk_index(d, j, n_chunks):
    return j + d * (n_chunks - 1 - 2 * j)


def _hgrn_kernel(*refs, has_s0, has_out):
    it = iter(refs)
    q_ref, f_ref, v_ref, lb_ref, cum_ref, pm_ref, qr_ref = (next(it) for _ in range(7))
    s0_ref = next(it) if has_s0 else None
    o_ref = next(it)
    sfin_ref = next(it) if has_out else None
    s_scr = next(it)
    j = pl.program_id(2)
    n_chunks = pl.num_programs(2)
    c = CHUNK

    @pl.when(j == 0)
    def _():
        if has_s0:
            s_scr[...] = s0_ref[0, 0, 0]
        else:
            s_scr[...] = jnp.zeros_like(s_scr)

    x = f_ref[...]
    logsig = jnp.minimum(x, 0.0) - jnp.log1p(jnp.exp(-jnp.abs(x)))
    la = lb_ref[0, 0:1, :]
    lc = lb_ref[0, 1:2, :] + logsig
    logf = jnp.maximum(la, lc) + jnp.log1p(jnp.exp(-jnp.abs(la - lc)))
    kf = jnp.exp(lc - x)
    q = _silu(q_ref[...]) * HEAD_DIM ** -0.5
    v = v_ref[...]
    sums = _dot_table(cum_ref[0], logf)
    eye = (lax.broadcasted_iota(jnp.int32, (c, c), 0) == lax.broadcasted_iota(jnp.int32, (c, c), 1))
    is_q = [qr_ref[0, lv] > 0.5 for lv in range(N_LEVELS)]

    for h in range(N_HEADS):
        hs = slice(h * HEAD_DIM, (h + 1) * HEAD_DIM)
        qh, kh, vh = q[:, hs], kf[:, hs], v[:, hs]
        scores = jnp.where(eye, _bdot_nt(qh, kh), 0.0)
        for lv in range(N_LEVELS):
            dq = sums[(2 + lv) * c:(3 + lv) * c, hs]
            e = jnp.exp(jnp.where(is_q[lv], dq, -dq))
            qt = jnp.where(is_q[lv], qh * e, 0.0)
            kt = jnp.where(is_q[lv], 0.0, kh * e)
            scores = scores + pm_ref[0, lv] * _bdot_nt(qt, kt)
        s_prev = s_scr[h]
        o_ref[0, :, hs] = _bdot(scores, vh) + _bdot(qh * jnp.exp(sums[0:c, hs]), s_prev)
        kdec = kh * jnp.exp(sums[c:2 * c, hs])
        btot = sums[c - 1:c, hs] + sums[2 * c - 1:2 * c, hs]
        e_col = jnp.transpose(jnp.broadcast_to(jnp.exp(btot), (HEAD_DIM, HEAD_DIM)))[:, 0:1]
        s_scr[h] = e_col * s_prev + _bdot_tn(kdec, vh)

    if has_out:
        @pl.when(j == n_chunks - 1)
        def _():
            sfin_ref[0, 0] = s_scr[...]


def _hgrn_call(z, lb_logs, tables, s0, layer, n_seq, seq, has_out):
    n_chunks = seq // CHUNK
    has_s0 = s0 is not None
    _, cum3, pmask, qrow = tables
    row = lambda d, b, j: b * n_chunks + _chunk_index(d, j, n_chunks)
    in_specs = [pl.BlockSpec((CHUNK, D_MODEL), lambda d, b, j: (row(d, b, j), COL_Q)),
                pl.BlockSpec((CHUNK, D_MODEL), lambda d, b, j: (row(d, b, j), COL_F + d)),
                pl.BlockSpec((CHUNK, D_MODEL), lambda d, b, j: (row(d, b, j), COL_I)),
                pl.BlockSpec((1, 2, D_MODEL), lambda d, b, j: (d, 0, 0)),
                pl.BlockSpec((1, (N_LEVELS + 2) * CHUNK, 3 * CHUNK), lambda d, b, j: (d, 0, 0)),
                pl.BlockSpec((1, N_LEVELS, CHUNK, CHUNK), lambda d, b, j: (d, 0, 0, 0)),
                pl.BlockSpec((1, N_LEVELS, CHUNK, 1), lambda d, b, j: (d, 0, 0, 0))]
    args = [z, z, z, lb_logs, cum3, pmask, qrow]
    if has_s0:
        in_specs.append(pl.BlockSpec((1, 1, 1, N_HEADS, HEAD_DIM, HEAD_DIM),
                                     lambda d, b, j: (b, layer, d, 0, 0, 0)))
        args.append(s0)
    out_specs = [pl.BlockSpec((1, CHUNK, D_MODEL), lambda d, b, j: (d, row(d, b, j), 0))]
    out_shape = [jax.ShapeDtypeStruct((N_DIR, n_seq * seq, D_MODEL), F32)]
    if has_out:
        out_specs.append(pl.BlockSpec((1, 1, N_HEADS, HEAD_DIM, HEAD_DIM), lambda d, b, j: (b, d, 0, 0, 0)))
        out_shape.append(jax.ShapeDtypeStruct((n_seq, N_DIR, N_HEADS, HEAD_DIM, HEAD_DIM), F32))
    res = pl.pallas_call(
        functools.partial(_hgrn_kernel, has_s0=has_s0, has_out=has_out),
        grid=(N_DIR, n_seq, n_chunks),
        in_specs=in_specs, out_specs=out_specs, out_shape=out_shape,
        scratch_shapes=[pltpu.VMEM((N_HEADS, HEAD_DIM, HEAD_DIM), F32)],
        compiler_params=_cparams(("parallel", "parallel", "arbitrary")),
        name="hgrn_scan",
    )(*args)
    return res if has_out else (res[0], None)


def _gdn_kernel(*refs, has_s0, has_out):
    it = iter(refs)
    qkv_refs = [next(it) for _ in range(3)]
    tail_ref, par_ref, tri_ref, pm_ref = next(it), next(it), next(it), next(it)
    s0_ref = next(it) if has_s0 else None
    o_ref = next(it)
    sfin_ref = next(it) if has_out else None
    s_scr = next(it)
    d = pl.program_id(0)
    j = pl.program_id(2)
    n_chunks = pl.num_programs(2)
    c = CHUNK

    @pl.when(j == 0)
    def _():
        if has_s0:
            s_scr[...] = s0_ref[0, 0, 0]
        else:
            s_scr[...] = jnp.zeros_like(s_scr)

    tail = tail_ref[...]
    beta_all = _sigmoid(tail)
    xg = tail + par_ref[1:2, :]
    softplus = jnp.maximum(xg, 0.0) + jnp.log1p(jnp.exp(-jnp.abs(xg)))
    g_all = par_ref[0:1, :] * softplus
    tri = tri_ref[0]
    tri3 = jnp.concatenate([tri, tri, tri], axis=1).astype(BF16)
    gcum = _dot_table(tri3, g_all)
    gcum_t = jnp.transpose(gcum)
    gtot = jnp.sum(g_all, axis=0, keepdims=True)
    incl = tri > 0.5
    eye = (lax.broadcasted_iota(jnp.int32, (c, c), 0) == lax.broadcasted_iota(jnp.int32, (c, c), 1))
    strict = jnp.logical_and(incl, jnp.logical_not(eye))
    eye_f = eye.astype(F32)
    q_all, k_all, v_all = (r[...] for r in qkv_refs)

    def pick(arr_fn, base, h):
        return jnp.where(d == 0, arr_fn(base + h), arr_fn(base + N_HEADS + h))

    a_mats, qks, rhss, g_cols, g_tots = [], [], [], [], []
    for h in range(N_HEADS):
        hs = slice(h * HEAD_DIM, (h + 1) * HEAD_DIM)
        qh, kh, vh = q_all[:, hs], k_all[:, hs], v_all[:, hs]
        beta = pick(lambda i: beta_all[:, i:i + 1], TAIL_BETA, h)
        g_c = pick(lambda i: gcum[:, i:i + 1], TAIL_DECAY, h)
        g_r = pick(lambda i: gcum_t[i:i + 1, :], TAIL_DECAY, h)
        g_t = pick(lambda i: gtot[:, i:i + 1], TAIL_DECAY, h)
        decay = jnp.where(incl, jnp.exp(jnp.where(incl, g_c - g_r, 0.0)), 0.0)
        qk_kk = _bdot_nt(jnp.concatenate([qh, kh], axis=0), kh) * jnp.concatenate([decay, decay], axis=0)
        qks.append(qk_kk[:c])
        a_mats.append(jnp.where(strict, beta * qk_kk[c:], 0.0))
        rhss.append(jnp.concatenate([vh * beta, kh * (beta * jnp.exp(g_c))], axis=1))
        g_cols.append(g_c)
        g_tots.append(g_t)

    t_invs = [eye_f - pm_ref[0, 0] * a for a in a_mats]
    for lv in range(1, N_LEVELS):
        pm = pm_ref[0, lv]
        halves = [_dot_x3(t, pm * a) for t, a in zip(t_invs, a_mats)]
        t_invs = [t - _dot_x3(e, t) for t, e in zip(t_invs, halves)]
    uws = [_dot_x3(t, r) for t, r in zip(t_invs, rhss)]

    for h in range(N_HEADS):
        hs = slice(h * HEAD_DIM, (h + 1) * HEAD_DIM)
        qh, kh = q_all[:, hs], k_all[:, hs]
        g_c, g_t = g_cols[h], g_tots[h]
        u, w = uws[h][:, :HEAD_DIM], uws[h][:, HEAD_DIM:]
        s_prev = s_scr[h]
        ws_qs = _bdot(jnp.concatenate([w, qh * jnp.exp(g_c)], axis=0), s_prev)
        v_new = u - ws_qs[:c]
        o_ref[0, :, hs] = ws_qs[c:] + _bdot(qks[h], v_new)
        kd = kh * jnp.exp(g_t - g_c)
        s_scr[h] = jnp.exp(g_t) * s_prev + _bdot_tn(kd, v_new)

    if has_out:
        @pl.when(j == n_chunks - 1)
        def _():
            sfin_ref[0, 0] = s_scr[...]


def _gdn_call(qkv, z, gdn_par, tri, pmask, s0, layer, n_seq, seq, has_out):
    n_chunks = seq // CHUNK
    has_s0 = s0 is not None
    row = lambda d, b, j: b * n_chunks + _chunk_index(d, j, n_chunks)
    in_specs = [pl.BlockSpec((CHUNK, D_MODEL), lambda d, b, j, p=p: (row(d, b, j), p)) for p in range(3)]
    in_specs += [pl.BlockSpec((CHUNK, 128), lambda d, b, j: (row(d, b, j), COL_TAIL // 128)),
                 pl.BlockSpec((2, 128), lambda d, b, j: (0, 0)),
                 pl.BlockSpec((1, CHUNK, CHUNK), lambda d, b, j: (d, 0, 0)),
                 pl.BlockSpec((1, N_LEVELS, CHUNK, CHUNK), lambda d, b, j: (d, 0, 0, 0))]
    args = [qkv, qkv, qkv, z, gdn_par, tri, pmask]
    if has_s0:
        in_specs.append(pl.BlockSpec((1, 1, 1, N_HEADS, HEAD_DIM, HEAD_DIM),
                                     lambda d, b, j: (b, layer, d, 0, 0, 0)))
        args.append(s0)
    out_specs = [pl.BlockSpec((1, CHUNK, D_MODEL), lambda d, b, j: (d, row(d, b, j), 0))]
    out_shape = [jax.ShapeDtypeStruct((N_DIR, n_seq * seq, D_MODEL), F32)]
    if has_out:
        out_specs.append(pl.BlockSpec((1, 1, N_HEADS, HEAD_DIM, HEAD_DIM), lambda d, b, j: (b, d, 0, 0, 0)))
        out_shape.append(jax.ShapeDtypeStruct((n_seq, N_DIR, N_HEADS, HEAD_DIM, HEAD_DIM), F32))
    res = pl.pallas_call(
        functools.partial(_gdn_kernel, has_s0=has_s0, has_out=has_out),
        grid=(N_DIR, n_seq, n_chunks),
        in_specs=in_specs, out_specs=out_specs, out_shape=out_shape,
        scratch_shapes=[pltpu.VMEM((N_HEADS, HEAD_DIM, HEAD_DIM), F32)],
        compiler_params=_cparams(("parallel", "parallel", "arbitrary")),
        name="gdn_scan",
    )(*args)
    return res if has_out else (res[0], None)


def _head_rmsnorm(x, gain):
    parts = []
    for h in range(N_HEADS):
        xh = x[:, h * HEAD_DIM:(h + 1) * HEAD_DIM]
        parts.append(xh * lax.rsqrt(jnp.mean(xh * xh, axis=-1, keepdims=True) + EPS))
    return jnp.concatenate(parts, axis=-1) * gain


def _post_kernel(oh_ref, og_ref, zoh_ref, zog_ref, zgh_ref, zgg_ref, x_ref, g1_ref, hn_ref, gn_ref,
                 wph_ref, wpg_ref, wo_ref, o_ref):
    oh = _head_rmsnorm(oh_ref[0] + oh_ref[1], hn_ref[...]) * _silu(zoh_ref[...])
    og = _head_rmsnorm(og_ref[0] + og_ref[1], gn_ref[...]) * _silu(zog_ref[...])
    y = (_sigmoid(zgh_ref[...]) * _bdot(oh, wph_ref[...])
         + _sigmoid(zgg_ref[...]) * _bdot(og, wpg_ref[...]))
    o_ref[...] = x_ref[...] + g1_ref[0] * _bdot(y, wo_ref[...])


def _post_call(oh, og, z, x, mod, hn, gn, wph, wpg, wo, rows_per_mod, tm):
    n = x.shape[0]
    mod_idx = lambda i: (i * tm) // rows_per_mod
    zspec = lambda col: pl.BlockSpec((tm, D_MODEL), lambda i, col=col: (i, col))
    wspec = pl.BlockSpec((D_MODEL, D_MODEL), lambda i: (0, 0))
    vspec = pl.BlockSpec((1, D_MODEL), lambda i: (0, 0))
    return pl.pallas_call(
        _post_kernel,
        grid=(n // tm,),
        in_specs=[pl.BlockSpec((N_DIR, tm, D_MODEL), lambda i: (0, i, 0)),
                  pl.BlockSpec((N_DIR, tm, D_MODEL), lambda i: (0, i, 0)),
                  zspec(COL_OGH), zspec(COL_OGG), zspec(COL_GATE_H), zspec(COL_GATE_G),
                  pl.BlockSpec((tm, D_MODEL), lambda i: (i, 0)),
                  pl.BlockSpec((1, 1, D_MODEL), lambda i: (mod_idx(i), 0, 2)),
                  vspec, vspec, wspec, wspec, wspec],
        out_specs=pl.BlockSpec((tm, D_MODEL), lambda i: (i, 0)),
        out_shape=jax.ShapeDtypeStruct((n, D_MODEL), F32),
        compiler_params=_cparams(("parallel",)),
        name="mix_out",
    )(oh, og, z, z, z, z, x, mod, hn, gn, wph, wpg, wo)


def _ffn_kernel(x_ref, sh_ref, sc_ref, g2_ref, gain_ref, wa_ref, wb_ref, cw_ref, cb_ref, wo_ref, *rest,
                rows, cols, final_norm):
    if final_norm:
        nf_ref, o_ref, h_scr = rest
    else:
        o_ref, h_scr = rest
    j = pl.program_id(1)
    nj = pl.num_programs(1)
    tr = x_ref.shape[0]

    @pl.when(j == 0)
    def _():
        h_scr[...] = _norm_mod(x_ref[...], gain_ref[...], sh_ref[0], sc_ref[0]).astype(BF16)

    h = h_scr[...]
    a = jnp.dot(h, wa_ref[...], preferred_element_type=F32)
    bgate = jnp.dot(h, wb_ref[...], preferred_element_type=F32)
    t = lax.broadcasted_iota(jnp.int32, (tr, 1), 0)
    col = t & (cols - 1)
    a_l = jnp.where(col >= 1, pltpu.roll(a, 1, 0), 0.0)
    a_r = jnp.where(col <= cols - 2, pltpu.roll(a, tr - 1, 0), 0.0)

    def hrow(dr):
        return a_l * cw_ref[dr, 0:1, :] + a * cw_ref[dr, 1:2, :] + a_r * cw_ref[dr, 2:3, :]

    conv = hrow(1)
    if rows > 1:
        r = (t >> int(math.log2(cols))) & (rows - 1)
        conv = conv + jnp.where(r >= 1, pltpu.roll(hrow(0), cols, 0), 0.0)
        conv = conv + jnp.where(r <= rows - 2, pltpu.roll(hrow(2), tr - cols, 0), 0.0)
    conv = conv + cb_ref[...]
    part = _bdot(_silu(conv) * bgate, wo_ref[...])

    @pl.when(j == 0)
    def _():
        o_ref[...] = part

    @pl.when(j > 0)
    def _():
        o_ref[...] += part

    @pl.when(j == nj - 1)
    def _():
        y = x_ref[...] + g2_ref[0] * o_ref[...]
        if final_norm:
            y = y * lax.rsqrt(jnp.mean(y * y, axis=-1, keepdims=True) + EPS) * nf_ref[...]
        o_ref[...] = y


def _ffn_call(x, mod, gain, w_in, cw, cb, w_out, rows_per_mod, tr, rows, cols, norm_final):
    n = x.shape[0]
    tf = 256
    nj = D_FF // tf
    mod_idx = lambda i: (i * tr) // rows_per_mod
    mspec = lambda k: pl.BlockSpec((1, 1, D_MODEL), lambda i, j, k=k: (mod_idx(i), 0, k))
    in_specs = [pl.BlockSpec((tr, D_MODEL), lambda i, j: (i, 0)),
                mspec(3), mspec(4), mspec(5),
                pl.BlockSpec((1, D_MODEL), lambda i, j: (0, 0)),
                pl.BlockSpec((D_MODEL, tf), lambda i, j: (0, j)),
                pl.BlockSpec((D_MODEL, tf), lambda i, j: (0, nj + j)),
                pl.BlockSpec((3, 3, tf), lambda i, j: (0, 0, j)),
                pl.BlockSpec((1, tf), lambda i, j: (0, j)),
                pl.BlockSpec((tf, D_MODEL), lambda i, j: (j, 0))]
    args = [x, mod, mod, mod, gain, w_in, w_in, cw, cb, w_out]
    final_norm = norm_final is not None
    if final_norm:
        in_specs.append(pl.BlockSpec((1, D_MODEL), lambda i, j: (0, 0)))
        args.append(norm_final)
    return pl.pallas_call(
        functools.partial(_ffn_kernel, rows=rows, cols=cols, final_norm=final_norm),
        grid=(n // tr, nj),
        in_specs=in_specs,
        out_specs=pl.BlockSpec((tr, D_MODEL), lambda i, j: (i, 0)),
        out_shape=jax.ShapeDtypeStruct((n, D_MODEL), F32),
        scratch_shapes=[pltpu.VMEM((tr, D_MODEL), BF16)],
        compiler_params=_cparams(("parallel", "arbitrary")),
        name="conv_ffn",
    )(*args)


def _run_pass(x, mod_of_layer, rows_per_mod, n_seq, seq, conv_rows, conv_cols, s_h0, s_g0, want_states,
              prm, tables):
    tri = tables[0]
    n = n_seq * seq
    assert conv_rows & (conv_rows - 1) == 0 and conv_cols & (conv_cols - 1) == 0
    ffn_rows = min(rows_per_mod, max(seq, 2048))
    assert ffn_rows % seq == 0 and n % ffn_rows == 0
    hs, gs = [], []
    for l in range(DEPTH):
        mod = mod_of_layer(l)
        z = _inproj_call(x, mod, prm["norm_mix"][l], prm["w_in"][l], rows_per_mod, tm=min(rows_per_mod, 1024), tn=1280)
        oh, s_h = _hgrn_call(z, prm["lb_logs"][l], tables, s_h0, l, n_seq, seq, want_states)
        qkv = _qkvconv_call(z, prm["gdn_conv"][l], n_seq, seq)
        og, s_g = _gdn_call(qkv, z, prm["gdn_par"][l], tri, tables[2], s_g0, l, n_seq, seq, want_states)
        x = _post_call(oh, og, z, x, mod, prm["hgrn_norm"][l], prm["gdn_norm"][l], prm["w_proj_h"][l],
                       prm["w_proj_g"][l], prm["w_out"][l], rows_per_mod, tm=256)
        x = _ffn_call(x, mod, prm["norm_ffn"][l], prm["w_ffn_in"][l], prm["ffn_conv"][l], prm["ffn_conv_b"][l],
                      prm["w_ffn_out"][l], rows_per_mod, ffn_rows, conv_rows, conv_cols,
                      prm["norm_final"] if l == DEPTH - 1 else None)
        hs.append(s_h)
        gs.append(s_g)
    return x, hs, gs


def kernel(x_prompt, x_sample, c, state_hgrn, state_gdn, c_ctx, w_ada, b_ada, norm_mix, norm_ffn, w_in, hgrn_lb, hgrn_norm, gdn_conv, gdn_a_log, gdn_dt_bias, gdn_norm, w_proj_h, w_proj_g, w_out, w_ffn_in, ffn_conv, ffn_conv_b, w_ffn_out, norm_final):
    ctx_b, ctx_len, _ = x_prompt.shape
    dec_b, dec_len, _ = x_sample.shape

    lb_cum = jnp.cumsum(jax.nn.softmax(hgrn_lb.astype(F32), axis=0), axis=0)
    lower = lb_cum - lb_cum[0]
    lb_logs = jnp.stack([jnp.log(lower), jnp.log1p(-lower)], axis=2)
    split = COL_GATE_H * D_MODEL
    small = 2 * N_DIR * N_HEADS
    w_in_r = jnp.concatenate(
        [w_in[..., :split], w_in[..., split + small:], w_in[..., split:split + small],
         jnp.zeros((DEPTH, D_MODEL, TAIL_W - small), w_in.dtype)], axis=-1).astype(BF16)
    gdn_par = jnp.zeros((DEPTH, 2, 128), F32)
    gdn_par = gdn_par.at[:, 0, TAIL_DECAY:TAIL_DECAY + N_DIR * N_HEADS].set(
        -jnp.exp(gdn_a_log.astype(F32)).reshape(DEPTH, -1))
    gdn_par = gdn_par.at[:, 1, TAIL_DECAY:TAIL_DECAY + N_DIR * N_HEADS].set(
        gdn_dt_bias.astype(F32).reshape(DEPTH, -1))
    prm = dict(
        norm_mix=norm_mix.reshape(DEPTH, 1, D_MODEL), norm_ffn=norm_ffn.reshape(DEPTH, 1, D_MODEL),
        w_in=w_in_r, lb_logs=lb_logs, gdn_conv=gdn_conv, gdn_par=gdn_par,
        hgrn_norm=jnp.tile(hgrn_norm, (1, N_HEADS)).reshape(DEPTH, 1, D_MODEL),
        gdn_norm=jnp.tile(gdn_norm, (1, N_HEADS)).reshape(DEPTH, 1, D_MODEL),
        w_proj_h=w_proj_h.astype(BF16), w_proj_g=w_proj_g.astype(BF16), w_out=w_out.astype(BF16),
        w_ffn_in=w_ffn_in.astype(BF16), ffn_conv=ffn_conv, ffn_conv_b=ffn_conv_b.reshape(DEPTH, 1, D_FF),
        w_ffn_out=w_ffn_out.astype(BF16), norm_final=norm_final.reshape(1, D_MODEL))
    tables = _scan_tables()

    cond = jnp.concatenate([c_ctx[None], c, jnp.zeros((16 - 1 - dec_b, D_MODEL), F32)], axis=0)
    mod = _ada_call(cond, w_ada, b_ada)

    n_ctx = ctx_b * ctx_len
    yc, hs, gs = _run_pass(x_prompt.reshape(n_ctx, D_MODEL), lambda l: mod[l, 0:1, None, :], n_ctx,
                           ctx_b, ctx_len, 1, ctx_len, None, None, True, prm, tables)
    n_dec = dec_b * dec_len
    yd, _, _ = _run_pass(x_sample.reshape(n_dec, D_MODEL), lambda l: mod[l, 1:1 + dec_b, None, :], dec_len,
                         dec_b, dec_len, dec_len // GRID_W, GRID_W, state_hgrn, state_gdn, False, prm, tables)
    return (yc.reshape(ctx_b, ctx_len, D_MODEL), yd.reshape(dec_b, dec_len, D_MODEL),
            jnp.stack(hs, axis=1), jnp.stack(gs, axis=1))
```

```python
import functools
import math

import jax
import jax.numpy as jnp
import numpy as np
from jax import lax
from jax.experimental import pallas as pl
from jax.experimental.pallas import tpu as pltpu

F32 = jnp.float32
BF16 = jnp.bfloat16

D_MODEL = 1024
DEPTH = 4
GRID_W = 64
N_HEADS = 8
HEAD_DIM = 128
D_FF = 2816
QKV_CONV = 5
N_DIR = 2
N_MOD = 6
EPS = 1e-6

CHUNK = 128
N_LEVELS = int(math.log2(CHUNK))

COL_Q, COL_F, COL_I, COL_OGH, COL_QKV, COL_OGG, COL_GATE_H, COL_GATE_G = 0, 1, 3, 4, 5, 8, 9, 10
COL_TAIL = 11 * D_MODEL
TAIL_W = 256
D_IN_PAD = COL_TAIL + TAIL_W
TAIL_BETA, TAIL_DECAY = 0, N_DIR * N_HEADS

VMEM_LIMIT_BYTES = 60 * 1024 * 1024


def _cparams(sem):
    return pltpu.CompilerParams(dimension_semantics=sem, vmem_limit_bytes=VMEM_LIMIT_BYTES)


def _bdot(a, b):
    return jnp.dot(a.astype(BF16), b.astype(BF16), preferred_element_type=F32)


def _bdot_nt(a, b):
    return lax.dot_general(a.astype(BF16), b.astype(BF16), (((1,), (1,)), ((), ())),
                           preferred_element_type=F32)


def _bdot_tn(a, b):
    return lax.dot_general(a.astype(BF16), b.astype(BF16), (((0,), (0,)), ((), ())),
                           preferred_element_type=F32)


def _sigmoid(x):
    return 1.0 / (1.0 + jnp.exp(-x))


def _silu(x):
    return x * _sigmoid(x)


def _ada_kernel(c_ref, w_ref, b_ref, o_ref):
    o_ref[0] = _bdot(_silu(c_ref[...]), w_ref[0]) + b_ref[0]


def _ada_call(cond, w_ada, b_ada):
    rows = cond.shape[0]
    tn = 1536
    return pl.pallas_call(
        _ada_kernel,
        grid=(DEPTH, N_MOD * D_MODEL // tn),
        in_specs=[pl.BlockSpec((rows, D_MODEL), lambda l, j: (0, 0)),
                  pl.BlockSpec((1, D_MODEL, tn), lambda l, j: (l, 0, j)),
                  pl.BlockSpec((1, 1, tn), lambda l, j: (l, 0, j))],
        out_specs=pl.BlockSpec((1, rows, tn), lambda l, j: (l, 0, j)),
        out_shape=jax.ShapeDtypeStruct((DEPTH, rows, N_MOD * D_MODEL), F32),
        compiler_params=_cparams(("parallel", "parallel")),
        name="ada_mod",
    )(cond, w_ada, b_ada.reshape(DEPTH, 1, N_MOD * D_MODEL))


def _norm_mod(x, gain, shift, scale):
    y = x * lax.rsqrt(jnp.mean(x * x, axis=-1, keepdims=True) + EPS)
    return (y * gain) * (1.0 + scale) + shift


def _inproj_kernel(x_ref, sh_ref, sc_ref, g_ref, w_ref, o_ref, h_scr):
    @pl.when(pl.program_id(1) == 0)
    def _():
        h_scr[...] = _norm_mod(x_ref[...], g_ref[...], sh_ref[0], sc_ref[0]).astype(BF16)

    o_ref[...] = jnp.dot(h_scr[...], w_ref[...], preferred_element_type=F32)


def _inproj_call(x, mod, gain, w, rows_per_mod, tm, tn):
    n = x.shape[0]
    mod_idx = lambda i: (i * tm) // rows_per_mod
    return pl.pallas_call(
        _inproj_kernel,
        grid=(n // tm, D_IN_PAD // tn),
        in_specs=[pl.BlockSpec((tm, D_MODEL), lambda i, j: (i, 0)),
                  pl.BlockSpec((1, 1, D_MODEL), lambda i, j: (mod_idx(i), 0, 0)),
                  pl.BlockSpec((1, 1, D_MODEL), lambda i, j: (mod_idx(i), 0, 1)),
                  pl.BlockSpec((1, D_MODEL), lambda i, j: (0, 0)),
                  pl.BlockSpec((D_MODEL, tn), lambda i, j: (0, j))],
        out_specs=pl.BlockSpec((tm, tn), lambda i, j: (i, j)),
        out_shape=jax.ShapeDtypeStruct((n, D_IN_PAD), F32),
        scratch_shapes=[pltpu.VMEM((tm, D_MODEL), BF16)],
        compiler_params=_cparams(("parallel", "arbitrary")),
        name="in_proj",
    )(x, mod, mod, gain, w)


def _qkvconv_kernel(z_ref, w_ref, o_ref, *, seq):
    jc = pl.program_id(1)
    x = z_ref[...]
    t = lax.broadcasted_iota(jnp.int32, (seq, 1), 0)
    pad = QKV_CONV // 2
    acc = x * w_ref[pad:pad + 1, :]
    for s in range(1, pad + 1):
        prev = jnp.where(t >= s, pltpu.roll(x, s, 0), 0.0)
        nxt = jnp.where(t < seq - s, pltpu.roll(x, seq - s, 0), 0.0)
        acc = acc + prev * w_ref[pad - s:pad - s + 1, :] + nxt * w_ref[pad + s:pad + s + 1, :]
    y = _silu(acc)
    tiles_per_part = D_MODEL // 256
    is_v = jc >= 2 * tiles_per_part
    scale = jnp.where(jc < tiles_per_part, HEAD_DIM ** -0.5, 1.0)
    for h in range(256 // HEAD_DIM):
        yh = y[:, h * HEAD_DIM:(h + 1) * HEAD_DIM]
        ss = jnp.sum(yh * yh, axis=-1, keepdims=True)
        normed = yh * (lax.rsqrt(ss + EPS) * scale)
        o_ref[:, h * HEAD_DIM:(h + 1) * HEAD_DIM] = jnp.where(is_v, yh, normed)


def _qkvconv_call(z, conv_w, n_seq, seq):
    tc = 256
    n_tiles = 3 * D_MODEL // tc
    base = COL_QKV * D_MODEL // tc
    return pl.pallas_call(
        functools.partial(_qkvconv_kernel, seq=seq),
        grid=(n_seq, n_tiles),
        in_specs=[pl.BlockSpec((seq, tc), lambda b, j: (b, base + j)),
                  pl.BlockSpec((QKV_CONV, tc), lambda b, j: (0, j))],
        out_specs=pl.BlockSpec((seq, tc), lambda b, j: (b, j)),
        out_shape=jax.ShapeDtypeStruct((n_seq * seq, 3 * D_MODEL), F32),
        compiler_params=_cparams(("parallel", "parallel")),
        name="qkv_conv",
    )(z, conv_w)


def _scan_tables():
    c = CHUNK
    idx = np.arange(c)
    tri = np.zeros((N_DIR, c, c), np.float32)
    tri[0] = (idx[None, :] <= idx[:, None])
    tri[1] = (idx[None, :] >= idx[:, None])
    pmask = np.zeros((N_DIR, N_LEVELS, c, c), np.float32)
    qrow = np.zeros((N_DIR, N_LEVELS, c, 1), np.float32)
    for lv in range(N_LEVELS):
        m = 1 << lv
        blk = idx // (2 * m)
        upper = (idx // m) % 2 == 1
        same = blk[:, None] == blk[None, :]
        pmask[0, lv] = same & upper[:, None] & ~upper[None, :]
        qrow[0, lv, :, 0] = upper
        pmask[1, lv] = same & ~upper[:, None] & upper[None, :]
        qrow[1, lv, :, 0] = ~upper
    return jnp.asarray(tri), jnp.asarray(pmask), jnp.asarray(qrow)


def _split3(x):
    hi = x.astype(BF16)
    r1 = x - hi.astype(F32)
    mid = r1.astype(BF16)
    lo = (r1 - mid.astype(F32)).astype(BF16)
    return hi, mid, lo


def _dot_table(table3, x):
    return jnp.dot(table3, jnp.concatenate(_split3(x), axis=0), preferred_element_type=F32)


def _level_reference_rows(b, lv, d):
    c, width = b.shape
    m = 1 << lv
    if 2 * m >= 8:
        b3 = b.reshape(c // (2 * m), 2 * m, width)
        ref = jnp.where(d == 0, b3[:, m - 1:m, :], b3[:, m:m + 1, :])
        return jnp.broadcast_to(ref, b3.shape).reshape(c, width)
    assert m == 2
    b3 = b.reshape(c // 8, 8, width)
    first = jnp.where(d == 0, b3[:, 1:2, :], b3[:, 2:3, :])
    second = jnp.where(d == 0, b3[:, 5:6, :], b3[:, 6:7, :])
    r = lax.broadcasted_iota(jnp.int32, (1, 8, 1), 1)
    return jnp.where(r < 4, first, second).reshape(c, width)


def _chunk_index(d, j, n_chunks):
    return j + d * (n_chunks - 1 - 2 * j)


SEQS_PER_STEP = 2


def _hgrn_stages(q_ref, f_ref, v_ref, lb_ref, tri3, pm_ref, qr_ref, o_ref, s_scr, d):
    for s in range(SEQS_PER_STEP):
        x = f_ref[s]
        logsig = jnp.minimum(x, 0.0) - jnp.log(1.0 + jnp.exp(-jnp.abs(x)))
        la = lb_ref[0, 0:1, :]
        lc = lb_ref[0, 1:2, :] + logsig
        logf = jnp.maximum(la, lc) + jnp.log(1.0 + jnp.exp(-jnp.abs(la - lc)))
        kf = jnp.exp(lc - x)
        q = _silu(q_ref[s]) * HEAD_DIM ** -0.5
        v = v_ref[s]
        yield
        b = _dot_table(tri3, logf)
        btot = jnp.sum(logf, axis=0, keepdims=True)
        xs = [jnp.where(qr_ref[0, 0] > 0.5, q * jnp.exp(logf), kf)]
        yield
        for lv in range(1, N_LEVELS):
            e = jnp.exp(-jnp.abs(b - _level_reference_rows(b, lv, d)))
            xs.append(jnp.where(qr_ref[0, lv] > 0.5, q, kf) * e)
            yield
        q_in = q * jnp.exp(b)
        k_out = kf * jnp.exp(btot - b)
        e_tot = jnp.exp(btot)
        yield
        for h in range(N_HEADS):
            hs = slice(h * HEAD_DIM, (h + 1) * HEAD_DIM)
            qh, kh, vh = q[:, hs], kf[:, hs], v[:, hs]
            scores = None
            for lv in range(N_LEVELS):
                xh = xs[lv][:, hs].astype(BF16)
                part = lax.dot_general(xh, xh, (((1,), (1,)), ((), ())), preferred_element_type=F32)
                scores = jnp.where(pm_ref[0, lv] > 0.5, part, 0.0 if scores is None else scores)
            s_prev = s_scr[s, h]
            diag = jnp.sum(qh * kh, axis=-1, keepdims=True)
            o_ref[0, s, :, hs] = _bdot(scores, vh) + diag * vh + _bdot(q_in[:, hs], s_prev)
            e_col = jnp.transpose(jnp.broadcast_to(e_tot[:, hs], (HEAD_DIM, HEAD_DIM)))[:, 0:1]
            s_scr[s, h] = e_col * s_prev + _bdot_tn(k_out[:, hs], vh)
            yield


def _gdn_stages(q_ref, k_ref, v_ref, tail_ref, par_ref, tri3, pm_ref, o_ref, s_scr, d):
    c = CHUNK
    incl = tri3[:, :c].astype(F32) > 0.5
    eye = (lax.broadcasted_iota(jnp.int32, (c, c), 0) == lax.broadcasted_iota(jnp.int32, (c, c), 1))
    strict = jnp.logical_and(incl, jnp.logical_not(eye))

    def pick(arr_fn, base, h):
        return jnp.where(d == 0, arr_fn(base + h), arr_fn(base + N_HEADS + h))

    a_mats, qks, rhss, g_cols, g_tots, where = [], [], [], [], [], []
    for s in range(SEQS_PER_STEP):
        tail = tail_ref[s]
        beta_all = _sigmoid(tail)
        xg = tail + par_ref[1:2, :]
        softplus = jnp.maximum(xg, 0.0) + jnp.log1p(jnp.exp(-jnp.abs(xg)))
        g_all = par_ref[0:1, :] * softplus
        gcum = _dot_table(tri3, g_all)
        gcum_t = jnp.transpose(gcum)
        gtot = jnp.sum(g_all, axis=0, keepdims=True)
        for h in range(N_HEADS):
            hs = slice(h * HEAD_DIM, (h + 1) * HEAD_DIM)
            qh, kh, vh = q_ref[s, :, hs], k_ref[s, :, hs], v_ref[s, :, hs]
            beta = pick(lambda i: beta_all[:, i:i + 1], TAIL_BETA, h)
            g_c = pick(lambda i: gcum[:, i:i + 1], TAIL_DECAY, h)
            g_r = pick(lambda i: gcum_t[i:i + 1, :], TAIL_DECAY, h)
            g_t = pick(lambda i: gtot[:, i:i + 1], TAIL_DECAY, h)
            decay = jnp.where(incl, jnp.exp(jnp.where(incl, g_c - g_r, 0.0)), 0.0)
            qk_kk = _bdot_nt(jnp.concatenate([qh, kh], axis=0), kh) * jnp.concatenate([decay, decay], axis=0)
            qks.append(qk_kk[:c])
            a_mats.append(jnp.where(strict, beta * qk_kk[c:], 0.0))
            rhss.append(jnp.concatenate([vh * beta, kh * (beta * jnp.exp(g_c))], axis=1))
            g_cols.append(g_c)
            g_tots.append(g_t)
            where.append((s, h))
            if h % 2 == 1:
                yield

    n_invs = [-(pm_ref[0, 0] * a) for a in a_mats]
    a_bf = [a.astype(BF16) for a in a_mats]
    for lv in range(1, N_LEVELS):
        pm = pm_ref[0, lv]
        firsts = [pm * (a + jnp.dot(n.astype(BF16), ab, preferred_element_type=F32))
                  for n, a, ab in zip(n_invs, a_mats, a_bf)]
        yield
        n_invs = [n - e - _bdot(e, n) for n, e in zip(n_invs, firsts)]
        yield
    uws = [r + _bdot(n, r) for n, r in zip(n_invs, rhss)]
    yield

    for i, (s, h) in enumerate(where):
        hs = slice(h * HEAD_DIM, (h + 1) * HEAD_DIM)
        qh, kh = q_ref[s, :, hs], k_ref[s, :, hs]
        g_c, g_t = g_cols[i], g_tots[i]
        u, w = uws[i][:, :HEAD_DIM], uws[i][:, HEAD_DIM:]
        s_prev = s_scr[s, h]
        ws_qs = _bdot(jnp.concatenate([w, qh * jnp.exp(g_c)], axis=0), s_prev)
        v_new = u - ws_qs[:c]
        o_ref[0, s, :, hs] = ws_qs[c:] + _bdot(qks[i], v_new)
        kd = kh * jnp.exp(g_t - g_c)
        s_scr[s, h] = jnp.exp(g_t) * s_prev + _bdot_tn(kd, v_new)
        yield


def _mixer_kernel(*refs, has_s0, has_out):
    it = iter(refs)
    zq_ref, zf_ref, zv_ref, lb_ref, tri3_ref, pm_ref, qr_ref = (next(it) for _ in range(7))
    gq_ref, gk_ref, gv_ref, tail_ref, par_ref = (next(it) for _ in range(5))
    s0h_ref, s0g_ref = (next(it), next(it)) if has_s0 else (None, None)
    oh_ref, og_ref = next(it), next(it)
    sfh_ref, sfg_ref = (next(it), next(it)) if has_out else (None, None)
    sh_scr, sg_scr = next(it), next(it)
    d = pl.program_id(0)

    @pl.when(pl.program_id(2) == 0)
    def _():
        if has_s0:
            sh_scr[...] = s0h_ref[:, 0, 0]
            sg_scr[...] = s0g_ref[:, 0, 0]
        else:
            sh_scr[...] = jnp.zeros_like(sh_scr)
            sg_scr[...] = jnp.zeros_like(sg_scr)

    tri3 = tri3_ref[0]
    pending = [_gdn_stages(gq_ref, gk_ref, gv_ref, tail_ref, par_ref, tri3, pm_ref, og_ref, sg_scr, d),
               _hgrn_stages(zq_ref, zf_ref, zv_ref, lb_ref, tri3, pm_ref, qr_ref, oh_ref, sh_scr, d)]
    while pending:
        for g in list(pending):
            try:
                next(g)
            except StopIteration:
                pending.remove(g)

    if has_out:
        @pl.when(pl.program_id(2) == pl.num_programs(2) - 1)
        def _():
            sfh_ref[:, 0] = sh_scr[...]
            sfg_ref[:, 0] = sg_scr[...]


def _mixer_call(z, qkv, lb_logs, gdn_par, tables, s_h0, s_g0, layer, n_seq, seq, has_out):
    ns = SEQS_PER_STEP
    n_chunks = seq // CHUNK
    has_s0 = s_h0 is not None
    tri, pmask, qrow = tables
    tri3 = jnp.concatenate([tri, tri, tri], axis=-1).astype(BF16)
    z3 = z.reshape(n_seq, seq, D_IN_PAD)
    qkv3 = qkv.reshape(n_seq, seq, 3 * D_MODEL)
    rows = lambda d, b, j: (b, _chunk_index(d, j, n_chunks))
    blk = lambda width, col: pl.BlockSpec((ns, CHUNK, width), lambda d, b, j: (*rows(d, b, j), col(d)))
    state_in = pl.BlockSpec((ns, 1, 1, N_HEADS, HEAD_DIM, HEAD_DIM), lambda d, b, j: (b, layer, d, 0, 0, 0))
    state_out = pl.BlockSpec((ns, 1, N_HEADS, HEAD_DIM, HEAD_DIM), lambda d, b, j: (b, d, 0, 0, 0))
    o_spec = pl.BlockSpec((1, ns, CHUNK, D_MODEL), lambda d, b, j: (d, *rows(d, b, j), 0))
    o_shape = jax.ShapeDtypeStruct((N_DIR, n_seq, seq, D_MODEL), F32)
    s_shape = jax.ShapeDtypeStruct((n_seq, N_DIR, N_HEADS, HEAD_DIM, HEAD_DIM), F32)
    in_specs = [blk(D_MODEL, lambda d: COL_Q), blk(D_MODEL, lambda d: COL_F + d), blk(D_MODEL, lambda d: COL_I),
                pl.BlockSpec((1, 2, D_MODEL), lambda d, b, j: (d, 0, 0)),
                pl.BlockSpec((1, CHUNK, 3 * CHUNK), lambda d, b, j: (d, 0, 0)),
                pl.BlockSpec((1, N_LEVELS, CHUNK, CHUNK), lambda d, b, j: (d, 0, 0, 0)),
                pl.BlockSpec((1, N_LEVELS, CHUNK, 1), lambda d, b, j: (d, 0, 0, 0)),
                blk(D_MODEL, lambda d: 0), blk(D_MODEL, lambda d: 1), blk(D_MODEL, lambda d: 2),
                blk(128, lambda d: COL_TAIL // 128),
                pl.BlockSpec((2, 128), lambda d, b, j: (0, 0))]
    args = [z3, z3, z3, lb_logs, tri3, pmask, qrow, qkv3, qkv3, qkv3, z3, gdn_par]
    if has_s0:
        in_specs += [state_in, state_in]
        args += [s_h0, s_g0]
    res = pl.pallas_call(
        functools.partial(_mixer_kernel, has_s0=has_s0, has_out=has_out),
        grid=(N_DIR, n_seq // ns, n_chunks),
        in_specs=in_specs,
        out_specs=[o_spec, o_spec] + ([state_out, state_out] if has_out else []),
        out_shape=[o_shape, o_shape] + ([s_shape, s_shape] if has_out else []),
        scratch_shapes=[pltpu.VMEM((ns, N_HEADS, HEAD_DIM, HEAD_DIM), F32)] * 2,
        compiler_params=_cparams(("parallel", "parallel", "arbitrary")),
        name="mixer_scan",
    )(*args)
    oh = res[0].reshape(N_DIR, n_seq * seq, D_MODEL)
    og = res[1].reshape(N_DIR, n_seq * seq, D_MODEL)
    return (oh, og, res[2], res[3]) if has_out else (oh, og, None, None)


def _head_rmsnorm(x, gain):
    parts = []
    for h in range(N_HEADS):
        xh = x[:, h * HEAD_DIM:(h + 1) * HEAD_DIM]
        parts.append(xh * lax.rsqrt(jnp.mean(xh * xh, axis=-1, keepdims=True) + EPS))
    return jnp.concatenate(parts, axis=-1) * gain


def _post_kernel(oh_ref, og_ref, zoh_ref, zog_ref, zgh_ref, zgg_ref, x_ref, g1_ref, hn_ref, gn_ref,
                 wph_ref, wpg_ref, wo_ref, o_ref):
    oh = _head_rmsnorm(oh_ref[0] + oh_ref[1], hn_ref[...]) * _silu(zoh_ref[...])
    og = _head_rmsnorm(og_ref[0] + og_ref[1], gn_ref[...]) * _silu(zog_ref[...])
    y = (_sigmoid(zgh_ref[...]) * _bdot(oh, wph_ref[...])
         + _sigmoid(zgg_ref[...]) * _bdot(og, wpg_ref[...]))
    o_ref[...] = x_ref[...] + g1_ref[0] * _bdot(y, wo_ref[...])


def _post_call(oh, og, z, x, mod, hn, gn, wph, wpg, wo, rows_per_mod, tm):
    n = x.shape[0]
    mod_idx = lambda i: (i * tm) // rows_per_mod
    zspec = lambda col: pl.BlockSpec((tm, D_MODEL), lambda i, col=col: (i, col))
    wspec = pl.BlockSpec((D_MODEL, D_MODEL), lambda i: (0, 0))
    vspec = pl.BlockSpec((1, D_MODEL), lambda i: (0, 0))
    return pl.pallas_call(
        _post_kernel,
        grid=(n // tm,),
        in_specs=[pl.BlockSpec((N_DIR, tm, D_MODEL), lambda i: (0, i, 0)),
                  pl.BlockSpec((N_DIR, tm, D_MODEL), lambda i: (0, i, 0)),
                  zspec(COL_OGH), zspec(COL_OGG), zspec(COL_GATE_H), zspec(COL_GATE_G),
                  pl.BlockSpec((tm, D_MODEL), lambda i: (i, 0)),
                  pl.BlockSpec((1, 1, D_MODEL), lambda i: (mod_idx(i), 0, 2)),
                  vspec, vspec, wspec, wspec, wspec],
        out_specs=pl.BlockSpec((tm, D_MODEL), lambda i: (i, 0)),
        out_shape=jax.ShapeDtypeStruct((n, D_MODEL), F32),
        compiler_params=_cparams(("parallel",)),
        name="mix_out",
    )(oh, og, z, z, z, z, x, mod, hn, gn, wph, wpg, wo)


def _ffn_up_kernel(x_ref, sh_ref, sc_ref, gain_ref, wa_ref, wb_ref, cw_ref, cb_ref, o_ref, h_scr, *, rows, cols):
    tr = x_ref.shape[0]

    @pl.when(pl.program_id(1) == 0)
    def _():
        h_scr[...] = _norm_mod(x_ref[...], gain_ref[...], sh_ref[0], sc_ref[0]).astype(BF16)

    h = h_scr[...]
    a = jnp.dot(h, wa_ref[...], preferred_element_type=F32)
    bgate = jnp.dot(h, wb_ref[...], preferred_element_type=F32)
    t = lax.broadcasted_iota(jnp.int32, (tr, 1), 0)
    col = t & (cols - 1)
    a_l = jnp.where(col >= 1, pltpu.roll(a, 1, 0), 0.0)
    a_r = jnp.where(col <= cols - 2, pltpu.roll(a, tr - 1, 0), 0.0)

    def hrow(dr):
        return a_l * cw_ref[dr, 0:1, :] + a * cw_ref[dr, 1:2, :] + a_r * cw_ref[dr, 2:3, :]

    conv = hrow(1)
    if rows > 1:
        r = (t >> int(math.log2(cols))) & (rows - 1)
        conv = conv + jnp.where(r >= 1, pltpu.roll(hrow(0), cols, 0), 0.0)
        conv = conv + jnp.where(r <= rows - 2, pltpu.roll(hrow(2), tr - cols, 0), 0.0)
    conv = conv + cb_ref[...]
    o_ref[...] = (_silu(conv) * bgate).astype(BF16)


def _ffn_down_kernel(a_ref, x_ref, g2_ref, wo_ref, *rest, final_norm):
    if final_norm:
        nf_ref, o_ref = rest
    else:
        (o_ref,) = rest
    y = x_ref[...] + g2_ref[0] * jnp.dot(a_ref[...], wo_ref[...], preferred_element_type=F32)
    if final_norm:
        y = y * lax.rsqrt(jnp.mean(y * y, axis=-1, keepdims=True) + EPS) * nf_ref[...]
    o_ref[...] = y


def _ffn_call(x, mod, gain, w_in, cw, cb, w_out, rows_per_mod, tr, rows, cols, norm_final):
    n = x.shape[0]
    tf = 256
    nj = D_FF // tf
    mod_idx = lambda i: (i * tr) // rows_per_mod
    mspec = lambda k: pl.BlockSpec((1, 1, D_MODEL), lambda i, j, k=k: (mod_idx(i), 0, k))
    act = pl.pallas_call(
        functools.partial(_ffn_up_kernel, rows=rows, cols=cols),
        grid=(n // tr, nj),
        in_specs=[pl.BlockSpec((tr, D_MODEL), lambda i, j: (i, 0)),
                  mspec(3), mspec(4),
                  pl.BlockSpec((1, D_MODEL), lambda i, j: (0, 0)),
                  pl.BlockSpec((D_MODEL, tf), lambda i, j: (0, j)),
                  pl.BlockSpec((D_MODEL, tf), lambda i, j: (0, nj + j)),
                  pl.BlockSpec((3, 3, tf), lambda i, j: (0, 0, j)),
                  pl.BlockSpec((1, tf), lambda i, j: (0, j))],
        out_specs=pl.BlockSpec((tr, tf), lambda i, j: (i, j)),
        out_shape=jax.ShapeDtypeStruct((n, D_FF), BF16),
        scratch_shapes=[pltpu.VMEM((tr, D_MODEL), BF16)],
        compiler_params=_cparams(("parallel", "arbitrary")),
        name="ffn_up",
    )(x, mod, mod, gain, w_in, w_in, cw, cb)

    tm = min(rows_per_mod, 512)
    mod_idx2 = lambda i: (i * tm) // rows_per_mod
    final_norm = norm_final is not None
    in_specs = [pl.BlockSpec((tm, D_FF), lambda i: (i, 0)),
                pl.BlockSpec((tm, D_MODEL), lambda i: (i, 0)),
                pl.BlockSpec((1, 1, D_MODEL), lambda i: (mod_idx2(i), 0, 5)),
                pl.BlockSpec((D_FF, D_MODEL), lambda i: (0, 0))]
    args = [act, x, mod, w_out]
    if final_norm:
        in_specs.append(pl.BlockSpec((1, D_MODEL), lambda i: (0, 0)))
        args.append(norm_final)
    return pl.pallas_call(
        functools.partial(_ffn_down_kernel, final_norm=final_norm),
        grid=(n // tm,),
        in_specs=in_specs,
        out_specs=pl.BlockSpec((tm, D_MODEL), lambda i: (i, 0)),
        out_shape=jax.ShapeDtypeStruct((n, D_MODEL), F32),
        compiler_params=_cparams(("parallel",)),
        name="ffn_down",
    )(*args)


def _run_pass(x, mod_of_layer, rows_per_mod, n_seq, seq, conv_rows, conv_cols, s_h0, s_g0, want_states,
              prm, tables):
    n = n_seq * seq
    assert conv_rows & (conv_rows - 1) == 0 and conv_cols & (conv_cols - 1) == 0
    ffn_rows = min(rows_per_mod, max(seq, 2048))
    assert ffn_rows % seq == 0 and n % ffn_rows == 0
    hs, gs = [], []
    for l in range(DEPTH):
        mod = mod_of_layer(l)
        z = _inproj_call(x, mod, prm["norm_mix"][l], prm["w_in"][l], rows_per_mod, tm=min(rows_per_mod, 1024), tn=1280)
        qkv = _qkvconv_call(z, prm["gdn_conv"][l], n_seq, seq)
        oh, og, s_h, s_g = _mixer_call(z, qkv, prm["lb_logs"][l], prm["gdn_par"][l], tables, s_h0, s_g0, l,
                                       n_seq, seq, want_states)
        x = _post_call(oh, og, z, x, mod, prm["hgrn_norm"][l], prm["gdn_norm"][l], prm["w_proj_h"][l],
                       prm["w_proj_g"][l], prm["w_out"][l], rows_per_mod, tm=256)
        x = _ffn_call(x, mod, prm["norm_ffn"][l], prm["w_ffn_in"][l], prm["ffn_conv"][l], prm["ffn_conv_b"][l],
                      prm["w_ffn_out"][l], rows_per_mod, ffn_rows, conv_rows, conv_cols,
                      prm["norm_final"] if l == DEPTH - 1 else None)
        hs.append(s_h)
        gs.append(s_g)
    return x, hs, gs


def kernel(x_prompt, x_sample, c, state_hgrn, state_gdn, c_ctx, w_ada, b_ada, norm_mix, norm_ffn, w_in, hgrn_lb, hgrn_norm, gdn_conv, gdn_a_log, gdn_dt_bias, gdn_norm, w_proj_h, w_proj_g, w_out, w_ffn_in, ffn_conv, ffn_conv_b, w_ffn_out, norm_final):
    ctx_b, ctx_len, _ = x_prompt.shape
    dec_b, dec_len, _ = x_sample.shape

    lb_cum = jnp.cumsum(jax.nn.softmax(hgrn_lb.astype(F32), axis=0), axis=0)
    lower = lb_cum - lb_cum[0]
    lb_logs = jnp.stack([jnp.log(lower), jnp.log1p(-lower)], axis=2)
    split = COL_GATE_H * D_MODEL
    small = 2 * N_DIR * N_HEADS
    w_in_r = jnp.concatenate(
        [w_in[..., :split], w_in[..., split + small:], w_in[..., split:split + small],
         jnp.zeros((DEPTH, D_MODEL, TAIL_W - small), w_in.dtype)], axis=-1).astype(BF16)
    gdn_par = jnp.zeros((DEPTH, 2, 128), F32)
    gdn_par = gdn_par.at[:, 0, TAIL_DECAY:TAIL_DECAY + N_DIR * N_HEADS].set(
        -jnp.exp(gdn_a_log.astype(F32)).reshape(DEPTH, -1))
    gdn_par = gdn_par.at[:, 1, TAIL_DECAY:TAIL_DECAY + N_DIR * N_HEADS].set(
        gdn_dt_bias.astype(F32).reshape(DEPTH, -1))
    prm = dict(
        norm_mix=norm_mix.reshape(DEPTH, 1, D_MODEL), norm_ffn=norm_ffn.reshape(DEPTH, 1, D_MODEL),
        w_in=w_in_r, lb_logs=lb_logs, gdn_conv=gdn_conv, gdn_par=gdn_par,
        hgrn_norm=jnp.tile(hgrn_norm, (1, N_HEADS)).reshape(DEPTH, 1, D_MODEL),
        gdn_norm=jnp.tile(gdn_norm, (1, N_HEADS)).reshape(DEPTH, 1, D_MODEL),
        w_proj_h=w_proj_h.astype(BF16), w_proj_g=w_proj_g.astype(BF16), w_out=w_out.astype(BF16),
        w_ffn_in=w_ffn_in.astype(BF16), ffn_conv=ffn_conv, ffn_conv_b=ffn_conv_b.reshape(DEPTH, 1, D_FF),
        w_ffn_out=w_ffn_out.astype(BF16), norm_final=norm_final.reshape(1, D_MODEL))
    tables = _scan_tables()

    cond = jnp.concatenate([c_ctx[None], c, jnp.zeros((16 - 1 - dec_b, D_MODEL), F32)], axis=0)
    mod = _ada_call(cond, w_ada, b_ada)

    n_ctx = ctx_b * ctx_len
    yc, hs, gs = _run_pass(x_prompt.reshape(n_ctx, D_MODEL), lambda l: mod[l, 0:1, None, :], n_ctx,
                           ctx_b, ctx_len, 1, ctx_len, None, None, True, prm, tables)
    n_dec = dec_b * dec_len
    yd, _, _ = _run_pass(x_sample.reshape(n_dec, D_MODEL), lambda l: mod[l, 1:1 + dec_b, None, :], dec_len,
                         dec_b, dec_len, dec_len // GRID_W, GRID_W, state_hgrn, state_gdn, False, prm, tables)
    return (yc.reshape(ctx_b, ctx_len, D_MODEL), yd.reshape(dec_b, dec_len, D_MODEL),
            jnp.stack(hs, axis=1), jnp.stack(gs, axis=1))
```

```python
import functools
import math

import jax
import jax.numpy as jnp
import numpy as np
from jax import lax
from jax.experimental import pallas as pl
from jax.experimental.pallas import tpu as pltpu

F32 = jnp.float32
BF16 = jnp.bfloat16

D_MODEL = 1024
DEPTH = 4
GRID_W = 64
N_HEADS = 8
HEAD_DIM = 128
D_FF = 2816
QKV_CONV = 5
N_DIR = 2
N_MOD = 6
EPS = 1e-6

CHUNK = 128
N_LEVELS = int(math.log2(CHUNK))

COL_QKV, COL_Q, COL_F, COL_I, COL_OGH, COL_OGG, COL_GATE_H, COL_GATE_G = 0, 3, 4, 6, 7, 8, 9, 10
COL_TAIL = 11 * D_MODEL
TAIL_W = 256
D_IN_PAD = COL_TAIL + TAIL_W
TAIL_BETA, TAIL_DECAY = 0, N_DIR * N_HEADS

VMEM_LIMIT_BYTES = 60 * 1024 * 1024


def _cparams(sem):
    return pltpu.CompilerParams(dimension_semantics=sem, vmem_limit_bytes=VMEM_LIMIT_BYTES)


def _bdot(a, b):
    return jnp.dot(a.astype(BF16), b.astype(BF16), preferred_element_type=F32)


def _bdot_nt(a, b):
    return lax.dot_general(a.astype(BF16), b.astype(BF16), (((1,), (1,)), ((), ())),
                           preferred_element_type=F32)


def _bdot_tn(a, b):
    return lax.dot_general(a.astype(BF16), b.astype(BF16), (((0,), (0,)), ((), ())),
                           preferred_element_type=F32)


def _sigmoid(x):
    return 1.0 / (1.0 + jnp.exp(-x))


def _silu(x):
    return x * _sigmoid(x)


def _ada_kernel(c_ref, w_ref, b_ref, o_ref):
    o_ref[0] = _bdot(_silu(c_ref[...]), w_ref[0]) + b_ref[0]


def _ada_call(cond, w_ada, b_ada):
    rows = cond.shape[0]
    tn = 1536
    return pl.pallas_call(
        _ada_kernel,
        grid=(DEPTH, N_MOD * D_MODEL // tn),
        in_specs=[pl.BlockSpec((rows, D_MODEL), lambda l, j: (0, 0)),
                  pl.BlockSpec((1, D_MODEL, tn), lambda l, j: (l, 0, j)),
                  pl.BlockSpec((1, 1, tn), lambda l, j: (l, 0, j))],
        out_specs=pl.BlockSpec((1, rows, tn), lambda l, j: (l, 0, j)),
        out_shape=jax.ShapeDtypeStruct((DEPTH, rows, N_MOD * D_MODEL), F32),
        compiler_params=_cparams(("parallel", "parallel")),
        name="ada_mod",
    )(cond, w_ada, b_ada.reshape(DEPTH, 1, N_MOD * D_MODEL))


def _norm_mod(x, gain, shift, scale):
    y = x * lax.rsqrt(jnp.mean(x * x, axis=-1, keepdims=True) + EPS)
    return (y * gain) * (1.0 + scale) + shift


INPROJ_TN = 768


def _inproj_kernel(x_ref, sh_ref, sc_ref, g_ref, w_ref, cw_ref, o_ref, h_scr, *, seq):
    j = pl.program_id(1)
    tm = x_ref.shape[0]
    conv_tiles = 3 * D_MODEL // INPROJ_TN

    @pl.when(j == 0)
    def _():
        h_scr[...] = _norm_mod(x_ref[...], g_ref[...], sh_ref[0], sc_ref[0]).astype(BF16)

    z = jnp.dot(h_scr[...], w_ref[...], preferred_element_type=F32)

    @pl.when(j >= conv_tiles)
    def _():
        o_ref[...] = z

    @pl.when(j < conv_tiles)
    def _():
        t = lax.broadcasted_iota(jnp.int32, (tm, 1), 0) & (seq - 1)
        pad = QKV_CONV // 2
        for i in range(INPROJ_TN // HEAD_DIM):
            cs = slice(i * HEAD_DIM, (i + 1) * HEAD_DIM)
            zi = z[:, cs]
            acc = zi * cw_ref[pad:pad + 1, cs]
            for s in range(1, pad + 1):
                prev = jnp.where(t >= s, pltpu.roll(zi, s, 0), 0.0)
                nxt = jnp.where(t < seq - s, pltpu.roll(zi, tm - s, 0), 0.0)
                acc = acc + prev * cw_ref[pad - s:pad - s + 1, cs] + nxt * cw_ref[pad + s:pad + s + 1, cs]
            y = _silu(acc)
            head = j * (INPROJ_TN // HEAD_DIM) + i
            scale = jnp.where(head < N_HEADS, HEAD_DIM ** -0.5, 1.0)
            normed = y * (lax.rsqrt(jnp.sum(y * y, axis=-1, keepdims=True) + EPS) * scale)
            o_ref[:, cs] = jnp.where(head >= 2 * N_HEADS, y, normed)


def _inproj_call(x, mod, gain, w, conv_w, rows_per_mod, seq):
    n = x.shape[0]
    tm = min(rows_per_mod, 2048)
    tn = INPROJ_TN
    assert tm % seq == 0 and n % tm == 0 and seq & (seq - 1) == 0
    conv_tiles = 3 * D_MODEL // tn
    mod_idx = lambda i: (i * tm) // rows_per_mod
    return pl.pallas_call(
        functools.partial(_inproj_kernel, seq=seq),
        grid=(n // tm, D_IN_PAD // tn),
        in_specs=[pl.BlockSpec((tm, D_MODEL), lambda i, j: (i, 0)),
                  pl.BlockSpec((1, 1, D_MODEL), lambda i, j: (mod_idx(i), 0, 0)),
                  pl.BlockSpec((1, 1, D_MODEL), lambda i, j: (mod_idx(i), 0, 1)),
                  pl.BlockSpec((1, D_MODEL), lambda i, j: (0, 0)),
                  pl.BlockSpec((D_MODEL, tn), lambda i, j: (0, j)),
                  pl.BlockSpec((QKV_CONV, tn), lambda i, j: (0, jnp.minimum(j, conv_tiles - 1)))],
        out_specs=pl.BlockSpec((tm, tn), lambda i, j: (i, j)),
        out_shape=jax.ShapeDtypeStruct((n, D_IN_PAD), F32),
        scratch_shapes=[pltpu.VMEM((tm, D_MODEL), BF16)],
        compiler_params=_cparams(("parallel", "arbitrary")),
        name="in_proj",
    )(x, mod, mod, gain, w, conv_w)


def _scan_tables():
    c = CHUNK
    idx = np.arange(c)
    tri = np.zeros((N_DIR, c, c), np.float32)
    tri[0] = (idx[None, :] <= idx[:, None])
    tri[1] = (idx[None, :] >= idx[:, None])
    pmask = np.zeros((N_DIR, N_LEVELS, c, c), np.float32)
    qrow = np.zeros((N_DIR, N_LEVELS, c, 1), np.float32)
    for lv in range(N_LEVELS):
        m = 1 << lv
        blk = idx // (2 * m)
        upper = (idx // m) % 2 == 1
        same = blk[:, None] == blk[None, :]
        pmask[0, lv] = same & upper[:, None] & ~upper[None, :]
        qrow[0, lv, :, 0] = upper
        pmask[1, lv] = same & ~upper[:, None] & upper[None, :]
        qrow[1, lv, :, 0] = ~upper
    return jnp.asarray(tri), jnp.asarray(pmask), jnp.asarray(qrow)


def _split3(x):
    hi = x.astype(BF16)
    r1 = x - hi.astype(F32)
    mid = r1.astype(BF16)
    lo = (r1 - mid.astype(F32)).astype(BF16)
    return hi, mid, lo


def _dot_table(table3, x):
    return jnp.dot(table3, jnp.concatenate(_split3(x), axis=0), preferred_element_type=F32)


def _level_reference_rows(b, lv, d):
    c, width = b.shape
    m = 1 << lv
    if 2 * m >= 8:
        b3 = b.reshape(c // (2 * m), 2 * m, width)
        ref = jnp.where(d == 0, b3[:, m - 1:m, :], b3[:, m:m + 1, :])
        return jnp.broadcast_to(ref, b3.shape).reshape(c, width)
    assert m == 2
    b3 = b.reshape(c // 8, 8, width)
    first = jnp.where(d == 0, b3[:, 1:2, :], b3[:, 2:3, :])
    second = jnp.where(d == 0, b3[:, 5:6, :], b3[:, 6:7, :])
    r = lax.broadcasted_iota(jnp.int32, (1, 8, 1), 1)
    return jnp.where(r < 4, first, second).reshape(c, width)


def _chunk_index(d, j, n_chunks):
    return j + d * (n_chunks - 1 - 2 * j)


SEQS_PER_STEP = 2


def _hgrn_stages(q_ref, f_ref, v_ref, lb_ref, tri3, pm_ref, qr_ref, o_ref, s_scr, d):
    for s in range(SEQS_PER_STEP):
        x = f_ref[s]
        logsig = jnp.minimum(x, 0.0) - jnp.log(1.0 + jnp.exp(-jnp.abs(x)))
        la = lb_ref[0, 0:1, :]
        lc = lb_ref[0, 1:2, :] + logsig
        logf = jnp.maximum(la, lc) + jnp.log(1.0 + jnp.exp(-jnp.abs(la - lc)))
        kf = jnp.exp(lc - x)
        q = _silu(q_ref[s]) * HEAD_DIM ** -0.5
        v = v_ref[s]
        yield
        b = _dot_table(tri3, logf)
        btot = jnp.sum(logf, axis=0, keepdims=True)
        xs = [jnp.where(qr_ref[0, 0] > 0.5, q * jnp.exp(logf), kf)]
        yield
        for lv in range(1, N_LEVELS):
            e = jnp.exp(-jnp.abs(b - _level_reference_rows(b, lv, d)))
            xs.append(jnp.where(qr_ref[0, lv] > 0.5, q, kf) * e)
            yield
        q_in = q * jnp.exp(b)
        k_out = kf * jnp.exp(btot - b)
        e_tot = jnp.exp(btot)
        yield
        for h in range(N_HEADS):
            hs = slice(h * HEAD_DIM, (h + 1) * HEAD_DIM)
            qh, kh, vh = q[:, hs], kf[:, hs], v[:, hs]
            scores = None
            for lv in range(N_LEVELS):
                xh = xs[lv][:, hs].astype(BF16)
                part = lax.dot_general(xh, xh, (((1,), (1,)), ((), ())), preferred_element_type=F32)
                scores = jnp.where(pm_ref[0, lv] > 0.5, part, 0.0 if scores is None else scores)
            s_prev = s_scr[s, h]
            diag = jnp.sum(qh * kh, axis=-1, keepdims=True)
            o_ref[0, s, :, hs] = _bdot(scores, vh) + diag * vh + _bdot(q_in[:, hs], s_prev)
            e_col = jnp.transpose(jnp.broadcast_to(e_tot[:, hs], (HEAD_DIM, HEAD_DIM)))[:, 0:1]
            s_scr[s, h] = e_col * s_prev + _bdot_tn(k_out[:, hs], vh)
            yield


def _gdn_stages(q_ref, k_ref, v_ref, tail_ref, par_ref, tri3, pm_ref, o_ref, s_scr, d):
    c = CHUNK
    incl = tri3[:, :c].astype(F32) > 0.5
    eye = (lax.broadcasted_iota(jnp.int32, (c, c), 0) == lax.broadcasted_iota(jnp.int32, (c, c), 1))
    strict = jnp.logical_and(incl, jnp.logical_not(eye))

    def pick(arr_fn, base, h):
        return jnp.where(d == 0, arr_fn(base + h), arr_fn(base + N_HEADS + h))

    a_mats, qks, rhss, g_cols, g_tots, where = [], [], [], [], [], []
    for s in range(SEQS_PER_STEP):
        tail = tail_ref[s]
        beta_all = _sigmoid(tail)
        xg = tail + par_ref[1:2, :]
        softplus = jnp.maximum(xg, 0.0) + jnp.log1p(jnp.exp(-jnp.abs(xg)))
        g_all = par_ref[0:1, :] * softplus
        gcum = _dot_table(tri3, g_all)
        gcum_t = jnp.transpose(gcum)
        gtot = jnp.sum(g_all, axis=0, keepdims=True)
        for h in range(N_HEADS):
            hs = slice(h * HEAD_DIM, (h + 1) * HEAD_DIM)
            qh, kh, vh = q_ref[s, :, hs], k_ref[s, :, hs], v_ref[s, :, hs]
            beta = pick(lambda i: beta_all[:, i:i + 1], TAIL_BETA, h)
            g_c = pick(lambda i: gcum[:, i:i + 1], TAIL_DECAY, h)
            g_r = pick(lambda i: gcum_t[i:i + 1, :], TAIL_DECAY, h)
            g_t = pick(lambda i: gtot[:, i:i + 1], TAIL_DECAY, h)
            decay = jnp.where(incl, jnp.exp(jnp.where(incl, g_c - g_r, 0.0)), 0.0)
            qk_kk = _bdot_nt(jnp.concatenate([qh, kh], axis=0), kh) * jnp.concatenate([decay, decay], axis=0)
            qks.append(qk_kk[:c])
            a_mats.append(jnp.where(strict, beta * qk_kk[c:], 0.0))
            rhss.append(jnp.concatenate([vh * beta, kh * (beta * jnp.exp(g_c))], axis=1))
            g_cols.append(g_c)
            g_tots.append(g_t)
            where.append((s, h))
            if h % 2 == 1:
                yield

    n_invs = [-(pm_ref[0, 0] * a) for a in a_mats]
    a_bf = [a.astype(BF16) for a in a_mats]
    for lv in range(1, N_LEVELS):
        pm = pm_ref[0, lv]
        firsts = [pm * (a + jnp.dot(n.astype(BF16), ab, preferred_element_type=F32))
                  for n, a, ab in zip(n_invs, a_mats, a_bf)]
        yield
        n_invs = [n - e - _bdot(e, n) for n, e in zip(n_invs, firsts)]
        yield
    uws = [r + _bdot(n, r) for n, r in zip(n_invs, rhss)]
    yield

    for i, (s, h) in enumerate(where):
        hs = slice(h * HEAD_DIM, (h + 1) * HEAD_DIM)
        qh, kh = q_ref[s, :, hs], k_ref[s, :, hs]
        g_c, g_t = g_cols[i], g_tots[i]
        u, w = uws[i][:, :HEAD_DIM], uws[i][:, HEAD_DIM:]
        s_prev = s_scr[s, h]
        ws_qs = _bdot(jnp.concatenate([w, qh * jnp.exp(g_c)], axis=0), s_prev)
        v_new = u - ws_qs[:c]
        o_ref[0, s, :, hs] = ws_qs[c:] + _bdot(qks[i], v_new)
        kd = kh * jnp.exp(g_t - g_c)
        s_scr[s, h] = jnp.exp(g_t) * s_prev + _bdot_tn(kd, v_new)
        yield


def _mixer_kernel(*refs, has_s0, has_out):
    it = iter(refs)
    zq_ref, zf_ref, zv_ref, lb_ref, tri3_ref, pm_ref, qr_ref = (next(it) for _ in range(7))
    gq_ref, gk_ref, gv_ref, tail_ref, par_ref = (next(it) for _ in range(5))
    s0h_ref, s0g_ref = (next(it), next(it)) if has_s0 else (None, None)
    oh_ref, og_ref = next(it), next(it)
    sfh_ref, sfg_ref = (next(it), next(it)) if has_out else (None, None)
    sh_scr, sg_scr = next(it), next(it)
    d = pl.program_id(0)

    @pl.when(pl.program_id(2) == 0)
    def _():
        if has_s0:
            sh_scr[...] = s0h_ref[:, 0, 0]
            sg_scr[...] = s0g_ref[:, 0, 0]
        else:
            sh_scr[...] = jnp.zeros_like(sh_scr)
            sg_scr[...] = jnp.zeros_like(sg_scr)

    tri3 = tri3_ref[0]
    pending = [_gdn_stages(gq_ref, gk_ref, gv_ref, tail_ref, par_ref, tri3, pm_ref, og_ref, sg_scr, d),
               _hgrn_stages(zq_ref, zf_ref, zv_ref, lb_ref, tri3, pm_ref, qr_ref, oh_ref, sh_scr, d)]
    while pending:
        for g in list(pending):
            try:
                next(g)
            except StopIteration:
                pending.remove(g)

    if has_out:
        @pl.when(pl.program_id(2) == pl.num_programs(2) - 1)
        def _():
            sfh_ref[:, 0] = sh_scr[...]
            sfg_ref[:, 0] = sg_scr[...]


def _mixer_call(z, lb_logs, gdn_par, tables, s_h0, s_g0, layer, n_seq, seq, has_out):
    ns = SEQS_PER_STEP
    n_chunks = seq // CHUNK
    has_s0 = s_h0 is not None
    tri, pmask, qrow = tables
    tri3 = jnp.concatenate([tri, tri, tri], axis=-1).astype(BF16)
    z3 = z.reshape(n_seq, seq, D_IN_PAD)
    rows = lambda d, b, j: (b, _chunk_index(d, j, n_chunks))
    blk = lambda width, col: pl.BlockSpec((ns, CHUNK, width), lambda d, b, j: (*rows(d, b, j), col(d)))
    state_in = pl.BlockSpec((ns, 1, 1, N_HEADS, HEAD_DIM, HEAD_DIM), lambda d, b, j: (b, layer, d, 0, 0, 0))
    state_out = pl.BlockSpec((ns, 1, N_HEADS, HEAD_DIM, HEAD_DIM), lambda d, b, j: (b, d, 0, 0, 0))
    o_spec = pl.BlockSpec((1, ns, CHUNK, D_MODEL), lambda d, b, j: (d, *rows(d, b, j), 0))
    o_shape = jax.ShapeDtypeStruct((N_DIR, n_seq, seq, D_MODEL), F32)
    s_shape = jax.ShapeDtypeStruct((n_seq, N_DIR, N_HEADS, HEAD_DIM, HEAD_DIM), F32)
    in_specs = [blk(D_MODEL, lambda d: COL_Q), blk(D_MODEL, lambda d: COL_F + d), blk(D_MODEL, lambda d: COL_I),
                pl.BlockSpec((1, 2, D_MODEL), lambda d, b, j: (d, 0, 0)),
                pl.BlockSpec((1, CHUNK, 3 * CHUNK), lambda d, b, j: (d, 0, 0)),
                pl.BlockSpec((1, N_LEVELS, CHUNK, CHUNK), lambda d, b, j: (d, 0, 0, 0)),
                pl.BlockSpec((1, N_LEVELS, CHUNK, 1), lambda d, b, j: (d, 0, 0, 0)),
                blk(D_MODEL, lambda d: COL_QKV), blk(D_MODEL, lambda d: COL_QKV + 1),
                blk(D_MODEL, lambda d: COL_QKV + 2),
                blk(128, lambda d: COL_TAIL // 128),
                pl.BlockSpec((2, 128), lambda d, b, j: (0, 0))]
    args = [z3, z3, z3, lb_logs, tri3, pmask, qrow, z3, z3, z3, z3, gdn_par]
    if has_s0:
        in_specs += [state_in, state_in]
        args += [s_h0, s_g0]
    res = pl.pallas_call(
        functools.partial(_mixer_kernel, has_s0=has_s0, has_out=has_out),
        grid=(N_DIR, n_seq // ns, n_chunks),
        in_specs=in_specs,
        out_specs=[o_spec, o_spec] + ([state_out, state_out] if has_out else []),
        out_shape=[o_shape, o_shape] + ([s_shape, s_shape] if has_out else []),
        scratch_shapes=[pltpu.VMEM((ns, N_HEADS, HEAD_DIM, HEAD_DIM), F32)] * 2,
        compiler_params=_cparams(("parallel", "parallel", "arbitrary")),
        name="mixer_scan",
    )(*args)
    oh = res[0].reshape(N_DIR, n_seq * seq, D_MODEL)
    og = res[1].reshape(N_DIR, n_seq * seq, D_MODEL)
    return (oh, og, res[2], res[3]) if has_out else (oh, og, None, None)


def _head_rmsnorm(x, gain):
    parts = []
    for h in range(N_HEADS):
        xh = x[:, h * HEAD_DIM:(h + 1) * HEAD_DIM]
        parts.append(xh * lax.rsqrt(jnp.mean(xh * xh, axis=-1, keepdims=True) + EPS))
    return jnp.concatenate(parts, axis=-1) * gain


def _post_kernel(oh_ref, og_ref, zoh_ref, zog_ref, zgh_ref, zgg_ref, x_ref, g1_ref, hn_ref, gn_ref,
                 wph_ref, wpg_ref, wo_ref, o_ref):
    oh = _head_rmsnorm(oh_ref[0] + oh_ref[1], hn_ref[...]) * _silu(zoh_ref[...])
    og = _head_rmsnorm(og_ref[0] + og_ref[1], gn_ref[...]) * _silu(zog_ref[...])
    y = (_sigmoid(zgh_ref[...]) * _bdot(oh, wph_ref[...])
         + _sigmoid(zgg_ref[...]) * _bdot(og, wpg_ref[...]))
    o_ref[...] = x_ref[...] + g1_ref[0] * _bdot(y, wo_ref[...])


def _post_call(oh, og, z, x, mod, hn, gn, wph, wpg, wo, rows_per_mod, tm):
    n = x.shape[0]
    mod_idx = lambda i: (i * tm) // rows_per_mod
    zspec = lambda col: pl.BlockSpec((tm, D_MODEL), lambda i, col=col: (i, col))
    wspec = pl.BlockSpec((D_MODEL, D_MODEL), lambda i: (0, 0))
    vspec = pl.BlockSpec((1, D_MODEL), lambda i: (0, 0))
    return pl.pallas_call(
        _post_kernel,
        grid=(n // tm,),
        in_specs=[pl.BlockSpec((N_DIR, tm, D_MODEL), lambda i: (0, i, 0)),
                  pl.BlockSpec((N_DIR, tm, D_MODEL), lambda i: (0, i, 0)),
                  zspec(COL_OGH), zspec(COL_OGG), zspec(COL_GATE_H), zspec(COL_GATE_G),
                  pl.BlockSpec((tm, D_MODEL), lambda i: (i, 0)),
                  pl.BlockSpec((1, 1, D_MODEL), lambda i: (mod_idx(i), 0, 2)),
                  vspec, vspec, wspec, wspec, wspec],
        out_specs=pl.BlockSpec((tm, D_MODEL), lambda i: (i, 0)),
        out_shape=jax.ShapeDtypeStruct((n, D_MODEL), F32),
        compiler_params=_cparams(("parallel",)),
        name="mix_out",
    )(oh, og, z, z, z, z, x, mod, hn, gn, wph, wpg, wo)


FFN_SUB = 256
FFN_TILE = 256


def _ffn_up_kernel(x_ref, sh_ref, sc_ref, gain_ref, wab_ref, cw_ref, cb_ref, o_ref, h_scr, *, rows, cols):
    tr = x_ref.shape[0]
    tf = o_ref.shape[1]
    sub = FFN_SUB
    n_sub = tr // sub

    @pl.when(pl.program_id(1) == 0)
    def _():
        h_scr[...] = _norm_mod(x_ref[...], gain_ref[...], sh_ref[0], sc_ref[0]).astype(BF16)

    t = lax.broadcasted_iota(jnp.int32, (sub, 1), 0)
    col = t & (cols - 1)
    first_col, last_col = col >= 1, col <= cols - 2
    w = [[cw_ref[dr, dc:dc + 1, :] for dc in range(3)] for dr in range(3)]
    bias = cb_ref[...]

    def project(i):
        ab = jnp.dot(h_scr[i * sub:(i + 1) * sub, :], wab_ref[...], preferred_element_type=F32)
        a, gate = ab[:, :tf], ab[:, tf:]
        a_l = jnp.where(first_col, pltpu.roll(a, 1, 0), 0.0)
        a_r = jnp.where(last_col, pltpu.roll(a, sub - 1, 0), 0.0)
        taps = [a_l * w[dr][0] + a * w[dr][1] + a_r * w[dr][2] for dr in ((0, 1, 2) if rows > 1 else (1,))]
        return taps, gate

    def finish(i, prev, cur, nxt):
        taps, gate = cur
        if rows > 1:
            up_src, mid, dn_src = taps
            r = ((t + i * sub) >> int(math.log2(cols))) & (rows - 1)
            zeros = jnp.zeros((cols, tf), F32)
            above = jnp.concatenate([prev[0][0][sub - cols:] if prev is not None else zeros, up_src[:sub - cols]], axis=0)
            below = jnp.concatenate([dn_src[cols:], nxt[0][2][:cols] if nxt is not None else zeros], axis=0)
            conv = mid + jnp.where(r >= 1, above, 0.0) + jnp.where(r <= rows - 2, below, 0.0)
        else:
            conv = taps[0]
        o_ref[i * sub:(i + 1) * sub, :] = (_silu(conv + bias) * gate).astype(BF16)

    blocks = [None] * n_sub
    for i in range(n_sub):
        blocks[i] = project(i)
        if i >= 1:
            finish(i - 1, blocks[i - 2] if i >= 2 else None, blocks[i - 1], blocks[i])
    finish(n_sub - 1, blocks[n_sub - 2] if n_sub >= 2 else None, blocks[n_sub - 1], None)


def _ffn_down_kernel(a_ref, x_ref, g2_ref, wo_ref, *rest, final_norm):
    if final_norm:
        nf_ref, o_ref = rest
    else:
        (o_ref,) = rest
    y = x_ref[...] + g2_ref[0] * jnp.dot(a_ref[...], wo_ref[...], preferred_element_type=F32)
    if final_norm:
        y = y * lax.rsqrt(jnp.mean(y * y, axis=-1, keepdims=True) + EPS) * nf_ref[...]
    o_ref[...] = y


def _ffn_call(x, mod, gain, w_ab, cw, cb, w_out, rows_per_mod, tr, rows, cols, norm_final):
    n = x.shape[0]
    tf = FFN_TILE
    nj = D_FF // tf
    mod_idx = lambda i: (i * tr) // rows_per_mod
    mspec = lambda k: pl.BlockSpec((1, 1, D_MODEL), lambda i, j, k=k: (mod_idx(i), 0, k))
    assert tr % FFN_SUB == 0 and FFN_SUB % cols == 0
    act = pl.pallas_call(
        functools.partial(_ffn_up_kernel, rows=rows, cols=cols),
        grid=(n // tr, nj),
        in_specs=[pl.BlockSpec((tr, D_MODEL), lambda i, j: (i, 0)),
                  mspec(3), mspec(4),
                  pl.BlockSpec((1, D_MODEL), lambda i, j: (0, 0)),
                  pl.BlockSpec((D_MODEL, 2 * tf), lambda i, j: (0, j)),
                  pl.BlockSpec((3, 3, tf), lambda i, j: (0, 0, j)),
                  pl.BlockSpec((1, tf), lambda i, j: (0, j))],
        out_specs=pl.BlockSpec((tr, tf), lambda i, j: (i, j)),
        out_shape=jax.ShapeDtypeStruct((n, D_FF), BF16),
        scratch_shapes=[pltpu.VMEM((tr, D_MODEL), BF16)],
        compiler_params=_cparams(("parallel", "arbitrary")),
        name="ffn_up",
    )(x, mod, mod, gain, w_ab, cw, cb)

    tm = min(rows_per_mod, 512)
    mod_idx2 = lambda i: (i * tm) // rows_per_mod
    final_norm = norm_final is not None
    in_specs = [pl.BlockSpec((tm, D_FF), lambda i: (i, 0)),
                pl.BlockSpec((tm, D_MODEL), lambda i: (i, 0)),
                pl.BlockSpec((1, 1, D_MODEL), lambda i: (mod_idx2(i), 0, 5)),
                pl.BlockSpec((D_FF, D_MODEL), lambda i: (0, 0))]
    args = [act, x, mod, w_out]
    if final_norm:
        in_specs.append(pl.BlockSpec((1, D_MODEL), lambda i: (0, 0)))
        args.append(norm_final)
    return pl.pallas_call(
        functools.partial(_ffn_down_kernel, final_norm=final_norm),
        grid=(n // tm,),
        in_specs=in_specs,
        out_specs=pl.BlockSpec((tm, D_MODEL), lambda i: (i, 0)),
        out_shape=jax.ShapeDtypeStruct((n, D_MODEL), F32),
        compiler_params=_cparams(("parallel",)),
        name="ffn_down",
    )(*args)


def _run_pass(x, mod_of_layer, rows_per_mod, n_seq, seq, conv_rows, conv_cols, s_h0, s_g0, want_states,
              prm, tables):
    n = n_seq * seq
    assert conv_rows & (conv_rows - 1) == 0 and conv_cols & (conv_cols - 1) == 0
    ffn_rows = min(rows_per_mod, max(seq, 2048))
    assert ffn_rows % seq == 0 and n % ffn_rows == 0
    hs, gs = [], []
    for l in range(DEPTH):
        mod = mod_of_layer(l)
        z = _inproj_call(x, mod, prm["norm_mix"][l], prm["w_in"][l], prm["gdn_conv"][l], rows_per_mod, seq)
        oh, og, s_h, s_g = _mixer_call(z, prm["lb_logs"][l], prm["gdn_par"][l], tables, s_h0, s_g0, l,
                                       n_seq, seq, want_states)
        x = _post_call(oh, og, z, x, mod, prm["hgrn_norm"][l], prm["gdn_norm"][l], prm["w_proj_h"][l],
                       prm["w_proj_g"][l], prm["w_out"][l], rows_per_mod, tm=256)
        x = _ffn_call(x, mod, prm["norm_ffn"][l], prm["w_ffn_in"][l], prm["ffn_conv"][l], prm["ffn_conv_b"][l],
                      prm["w_ffn_out"][l], rows_per_mod, ffn_rows, conv_rows, conv_cols,
                      prm["norm_final"] if l == DEPTH - 1 else None)
        hs.append(s_h)
        gs.append(s_g)
    return x, hs, gs


def kernel(x_prompt, x_sample, c, state_hgrn, state_gdn, c_ctx, w_ada, b_ada, norm_mix, norm_ffn, w_in, hgrn_lb, hgrn_norm, gdn_conv, gdn_a_log, gdn_dt_bias, gdn_norm, w_proj_h, w_proj_g, w_out, w_ffn_in, ffn_conv, ffn_conv_b, w_ffn_out, norm_final):
    ctx_b, ctx_len, _ = x_prompt.shape
    dec_b, dec_len, _ = x_sample.shape

    lb_cum = jnp.cumsum(jax.nn.softmax(hgrn_lb.astype(F32), axis=0), axis=0)
    lower = lb_cum - lb_cum[0]
    lb_logs = jnp.stack([jnp.log(lower), jnp.log1p(-lower)], axis=2)
    hgrn_w, qkv_w = 5 * D_MODEL, 3 * D_MODEL
    small = 2 * N_DIR * N_HEADS
    tail_at = hgrn_w + qkv_w + D_MODEL
    w_in_r = jnp.concatenate(
        [w_in[..., hgrn_w:hgrn_w + qkv_w], w_in[..., :hgrn_w], w_in[..., hgrn_w + qkv_w:tail_at],
         w_in[..., tail_at + small:], w_in[..., tail_at:tail_at + small],
         jnp.zeros((DEPTH, D_MODEL, TAIL_W - small), w_in.dtype)], axis=-1).astype(BF16)
    nj = D_FF // FFN_TILE
    w_ffn_ab = jnp.concatenate([w_ffn_in[..., :D_FF].reshape(DEPTH, D_MODEL, nj, FFN_TILE),
                                w_ffn_in[..., D_FF:].reshape(DEPTH, D_MODEL, nj, FFN_TILE)],
                               axis=-1).reshape(DEPTH, D_MODEL, 2 * D_FF).astype(BF16)
    gdn_par = jnp.zeros((DEPTH, 2, 128), F32)
    gdn_par = gdn_par.at[:, 0, TAIL_DECAY:TAIL_DECAY + N_DIR * N_HEADS].set(
        -jnp.exp(gdn_a_log.astype(F32)).reshape(DEPTH, -1))
    gdn_par = gdn_par.at[:, 1, TAIL_DECAY:TAIL_DECAY + N_DIR * N_HEADS].set(
        gdn_dt_bias.astype(F32).reshape(DEPTH, -1))
    prm = dict(
        norm_mix=norm_mix.reshape(DEPTH, 1, D_MODEL), norm_ffn=norm_ffn.reshape(DEPTH, 1, D_MODEL),
        w_in=w_in_r, lb_logs=lb_logs, gdn_conv=gdn_conv, gdn_par=gdn_par,
        hgrn_norm=jnp.tile(hgrn_norm, (1, N_HEADS)).reshape(DEPTH, 1, D_MODEL),
        gdn_norm=jnp.tile(gdn_norm, (1, N_HEADS)).reshape(DEPTH, 1, D_MODEL),
        w_proj_h=w_proj_h.astype(BF16), w_proj_g=w_proj_g.astype(BF16), w_out=w_out.astype(BF16),
        w_ffn_in=w_ffn_ab, ffn_conv=ffn_conv, ffn_conv_b=ffn_conv_b.reshape(DEPTH, 1, D_FF),
        w_ffn_out=w_ffn_out.astype(BF16), norm_final=norm_final.reshape(1, D_MODEL))
    tables = _scan_tables()

    cond = jnp.concatenate([c_ctx[None], c, jnp.zeros((16 - 1 - dec_b, D_MODEL), F32)], axis=0)
    mod = _ada_call(cond, w_ada, b_ada)

    n_ctx = ctx_b * ctx_len
    yc, hs, gs = _run_pass(x_prompt.reshape(n_ctx, D_MODEL), lambda l: mod[l, 0:1, None, :], n_ctx,
                           ctx_b, ctx_len, 1, ctx_len, None, None, True, prm, tables)
    n_dec = dec_b * dec_len
    yd, _, _ = _run_pass(x_sample.reshape(n_dec, D_MODEL), lambda l: mod[l, 1:1 + dec_b, None, :], dec_len,
                         dec_b, dec_len, dec_len // GRID_W, GRID_W, state_hgrn, state_gdn, False, prm, tables)
    return (yc.reshape(ctx_b, ctx_len, D_MODEL), yd.reshape(dec_b, dec_len, D_MODEL),
            jnp.stack(hs, axis=1), jnp.stack(gs, axis=1))
```

```python
import functools
import math

import jax
import jax.numpy as jnp
import numpy as np
from jax import lax
from jax.experimental import pallas as pl
from jax.experimental.pallas import tpu as pltpu

F32 = jnp.float32
BF16 = jnp.bfloat16

D_MODEL = 1024
DEPTH = 4
GRID_W = 64
N_HEADS = 8
HEAD_DIM = 128
D_FF = 2816
QKV_CONV = 5
N_DIR = 2
N_MOD = 6
EPS = 1e-6

CHUNK = 128
N_LEVELS = int(math.log2(CHUNK))

COL_QKV, COL_Q, COL_F, COL_I, COL_OGH, COL_OGG, COL_GATE_H, COL_GATE_G = 0, 3, 4, 6, 7, 8, 9, 10
COL_TAIL = 11 * D_MODEL
TAIL_W = 256
D_IN_PAD = COL_TAIL + TAIL_W
TAIL_BETA, TAIL_DECAY = 0, N_DIR * N_HEADS

VMEM_LIMIT_BYTES = 60 * 1024 * 1024


def _cparams(sem):
    return pltpu.CompilerParams(dimension_semantics=sem, vmem_limit_bytes=VMEM_LIMIT_BYTES)


def _bdot(a, b):
    return jnp.dot(a.astype(BF16), b.astype(BF16), preferred_element_type=F32)


def _bdot_nt(a, b):
    return lax.dot_general(a.astype(BF16), b.astype(BF16), (((1,), (1,)), ((), ())),
                           preferred_element_type=F32)


def _bdot_tn(a, b):
    return lax.dot_general(a.astype(BF16), b.astype(BF16), (((0,), (0,)), ((), ())),
                           preferred_element_type=F32)


def _sigmoid(x):
    return 1.0 / (1.0 + jnp.exp(-x))


def _silu(x):
    return x * _sigmoid(x)


def _ada_kernel(c_ref, w_ref, b_ref, o_ref):
    o_ref[0] = _bdot(_silu(c_ref[...]), w_ref[0]) + b_ref[0]


def _ada_call(cond, w_ada, b_ada):
    rows = cond.shape[0]
    tn = 1536
    return pl.pallas_call(
        _ada_kernel,
        grid=(DEPTH, N_MOD * D_MODEL // tn),
        in_specs=[pl.BlockSpec((rows, D_MODEL), lambda l, j: (0, 0)),
                  pl.BlockSpec((1, D_MODEL, tn), lambda l, j: (l, 0, j)),
                  pl.BlockSpec((1, 1, tn), lambda l, j: (l, 0, j))],
        out_specs=pl.BlockSpec((1, rows, tn), lambda l, j: (l, 0, j)),
        out_shape=jax.ShapeDtypeStruct((DEPTH, rows, N_MOD * D_MODEL), F32),
        compiler_params=_cparams(("parallel", "parallel")),
        name="ada_mod",
    )(cond, w_ada, b_ada.reshape(DEPTH, 1, N_MOD * D_MODEL))


def _norm_mod(x, gain, shift, scale):
    y = x * lax.rsqrt(jnp.mean(x * x, axis=-1, keepdims=True) + EPS)
    return (y * gain) * (1.0 + scale) + shift


INPROJ_TN = 768


def _inproj_kernel(x_ref, sh_ref, sc_ref, g_ref, w_ref, cw_ref, o_ref, h_scr, *, seq):
    j = pl.program_id(1)
    tm = x_ref.shape[0]
    conv_tiles = 3 * D_MODEL // INPROJ_TN

    @pl.when(j == 0)
    def _():
        h_scr[...] = _norm_mod(x_ref[...], g_ref[...], sh_ref[0], sc_ref[0]).astype(BF16)

    z = jnp.dot(h_scr[...], w_ref[...], preferred_element_type=F32)

    @pl.when(j >= conv_tiles)
    def _():
        o_ref[...] = z

    @pl.when(j < conv_tiles)
    def _():
        t = lax.broadcasted_iota(jnp.int32, (tm, 1), 0) & (seq - 1)
        pad = QKV_CONV // 2
        for i in range(INPROJ_TN // HEAD_DIM):
            cs = slice(i * HEAD_DIM, (i + 1) * HEAD_DIM)
            zi = z[:, cs]
            acc = zi * cw_ref[pad:pad + 1, cs]
            for s in range(1, pad + 1):
                prev = jnp.where(t >= s, pltpu.roll(zi, s, 0), 0.0)
                nxt = jnp.where(t < seq - s, pltpu.roll(zi, tm - s, 0), 0.0)
                acc = acc + prev * cw_ref[pad - s:pad - s + 1, cs] + nxt * cw_ref[pad + s:pad + s + 1, cs]
            y = _silu(acc)
            head = j * (INPROJ_TN // HEAD_DIM) + i
            scale = jnp.where(head < N_HEADS, HEAD_DIM ** -0.5, 1.0)
            normed = y * (lax.rsqrt(jnp.sum(y * y, axis=-1, keepdims=True) + EPS) * scale)
            o_ref[:, cs] = jnp.where(head >= 2 * N_HEADS, y, normed)


def _inproj_call(x, mod, gain, w, conv_w, rows_per_mod, seq):
    n = x.shape[0]
    tm = min(rows_per_mod, 2048)
    tn = INPROJ_TN
    assert tm % seq == 0 and n % tm == 0 and seq & (seq - 1) == 0
    conv_tiles = 3 * D_MODEL // tn
    mod_idx = lambda i: (i * tm) // rows_per_mod
    return pl.pallas_call(
        functools.partial(_inproj_kernel, seq=seq),
        grid=(n // tm, D_IN_PAD // tn),
        in_specs=[pl.BlockSpec((tm, D_MODEL), lambda i, j: (i, 0)),
                  pl.BlockSpec((1, 1, D_MODEL), lambda i, j: (mod_idx(i), 0, 0)),
                  pl.BlockSpec((1, 1, D_MODEL), lambda i, j: (mod_idx(i), 0, 1)),
                  pl.BlockSpec((1, D_MODEL), lambda i, j: (0, 0)),
                  pl.BlockSpec((D_MODEL, tn), lambda i, j: (0, j)),
                  pl.BlockSpec((QKV_CONV, tn), lambda i, j: (0, jnp.minimum(j, conv_tiles - 1)))],
        out_specs=pl.BlockSpec((tm, tn), lambda i, j: (i, j)),
        out_shape=jax.ShapeDtypeStruct((n, D_IN_PAD), F32),
        scratch_shapes=[pltpu.VMEM((tm, D_MODEL), BF16)],
        compiler_params=_cparams(("parallel", "arbitrary")),
        name="in_proj",
    )(x, mod, mod, gain, w, conv_w)


def _scan_tables():
    c = CHUNK
    idx = np.arange(c)
    tri = np.zeros((N_DIR, c, c), np.float32)
    tri[0] = (idx[None, :] <= idx[:, None])
    tri[1] = (idx[None, :] >= idx[:, None])
    pmask = np.zeros((N_DIR, N_LEVELS, c, c), np.float32)
    qrow = np.zeros((N_DIR, N_LEVELS, c, 1), np.float32)
    for lv in range(N_LEVELS):
        m = 1 << lv
        blk = idx // (2 * m)
        upper = (idx // m) % 2 == 1
        same = blk[:, None] == blk[None, :]
        pmask[0, lv] = same & upper[:, None] & ~upper[None, :]
        qrow[0, lv, :, 0] = upper
        pmask[1, lv] = same & ~upper[:, None] & upper[None, :]
        qrow[1, lv, :, 0] = ~upper
    return jnp.asarray(tri), jnp.asarray(pmask), jnp.asarray(qrow)


def _split3(x):
    hi = x.astype(BF16)
    r1 = x - hi.astype(F32)
    mid = r1.astype(BF16)
    lo = (r1 - mid.astype(F32)).astype(BF16)
    return hi, mid, lo


def _dot_table(table3, x):
    return jnp.dot(table3, jnp.concatenate(_split3(x), axis=0), preferred_element_type=F32)


def _level_reference_rows(b, lv, d):
    c, width = b.shape
    m = 1 << lv
    if 2 * m >= 8:
        b3 = b.reshape(c // (2 * m), 2 * m, width)
        ref = jnp.where(d == 0, b3[:, m - 1:m, :], b3[:, m:m + 1, :])
        return jnp.broadcast_to(ref, b3.shape).reshape(c, width)
    assert m == 2
    b3 = b.reshape(c // 8, 8, width)
    first = jnp.where(d == 0, b3[:, 1:2, :], b3[:, 2:3, :])
    second = jnp.where(d == 0, b3[:, 5:6, :], b3[:, 6:7, :])
    r = lax.broadcasted_iota(jnp.int32, (1, 8, 1), 1)
    return jnp.where(r < 4, first, second).reshape(c, width)


def _chunk_index(d, j, n_chunks):
    return j + d * (n_chunks - 1 - 2 * j)


SEQS_PER_STEP = 2


def _hgrn_stages(q_ref, f_ref, v_ref, lb_ref, tri3, pm_ref, qr_ref, o_ref, s_scr, d):
    for s in range(SEQS_PER_STEP):
        x = f_ref[s]
        logsig = jnp.minimum(x, 0.0) - jnp.log(1.0 + jnp.exp(-jnp.abs(x)))
        la = lb_ref[0, 0:1, :]
        lc = lb_ref[0, 1:2, :] + logsig
        logf = jnp.maximum(la, lc) + jnp.log(1.0 + jnp.exp(-jnp.abs(la - lc)))
        kf = jnp.exp(lc - x)
        q = _silu(q_ref[s]) * HEAD_DIM ** -0.5
        v = v_ref[s]
        yield
        b = _dot_table(tri3, logf)
        btot = jnp.sum(logf, axis=0, keepdims=True)
        xs = [jnp.where(qr_ref[0, 0] > 0.5, q * jnp.exp(logf), kf)]
        yield
        for lv in range(1, N_LEVELS):
            e = jnp.exp(-jnp.abs(b - _level_reference_rows(b, lv, d)))
            xs.append(jnp.where(qr_ref[0, lv] > 0.5, q, kf) * e)
            yield
        q_in = q * jnp.exp(b)
        k_out = kf * jnp.exp(btot - b)
        e_tot = jnp.exp(btot)
        yield
        for h in range(N_HEADS):
            hs = slice(h * HEAD_DIM, (h + 1) * HEAD_DIM)
            qh, kh, vh = q[:, hs], kf[:, hs], v[:, hs]
            scores = None
            for lv in range(N_LEVELS):
                xh = xs[lv][:, hs].astype(BF16)
                part = lax.dot_general(xh, xh, (((1,), (1,)), ((), ())), preferred_element_type=F32)
                scores = jnp.where(pm_ref[0, lv] > 0.5, part, 0.0 if scores is None else scores)
            s_prev = s_scr[s, h]
            diag = jnp.sum(qh * kh, axis=-1, keepdims=True)
            o_ref[0, s, :, hs] = (_bdot(scores, vh) + diag * vh + _bdot(q_in[:, hs], s_prev)).astype(BF16)
            e_col = jnp.transpose(jnp.broadcast_to(e_tot[:, hs], (HEAD_DIM, HEAD_DIM)))[:, 0:1]
            s_scr[s, h] = e_col * s_prev + _bdot_tn(k_out[:, hs], vh)
            yield


def _gdn_stages(q_ref, k_ref, v_ref, tail_ref, par_ref, tri3, pm_ref, o_ref, s_scr, d):
    c = CHUNK
    incl = tri3[:, :c].astype(F32) > 0.5
    eye = (lax.broadcasted_iota(jnp.int32, (c, c), 0) == lax.broadcasted_iota(jnp.int32, (c, c), 1))
    strict = jnp.logical_and(incl, jnp.logical_not(eye))

    def pick(arr_fn, base, h):
        return jnp.where(d == 0, arr_fn(base + h), arr_fn(base + N_HEADS + h))

    a_mats, qks, rhss, g_cols, g_tots, where = [], [], [], [], [], []
    for s in range(SEQS_PER_STEP):
        tail = tail_ref[s]
        beta_all = _sigmoid(tail)
        xg = tail + par_ref[1:2, :]
        softplus = jnp.maximum(xg, 0.0) + jnp.log1p(jnp.exp(-jnp.abs(xg)))
        g_all = par_ref[0:1, :] * softplus
        gcum = _dot_table(tri3, g_all)
        gcum_t = jnp.transpose(gcum)
        gtot = jnp.sum(g_all, axis=0, keepdims=True)
        for h in range(N_HEADS):
            hs = slice(h * HEAD_DIM, (h + 1) * HEAD_DIM)
            qh, kh, vh = q_ref[s, :, hs], k_ref[s, :, hs], v_ref[s, :, hs]
            beta = pick(lambda i: beta_all[:, i:i + 1], TAIL_BETA, h)
            g_c = pick(lambda i: gcum[:, i:i + 1], TAIL_DECAY, h)
            g_r = pick(lambda i: gcum_t[i:i + 1, :], TAIL_DECAY, h)
            g_t = pick(lambda i: gtot[:, i:i + 1], TAIL_DECAY, h)
            decay = jnp.where(incl, jnp.exp(jnp.where(incl, g_c - g_r, 0.0)), 0.0)
            qk_kk = _bdot_nt(jnp.concatenate([qh, kh], axis=0), kh) * jnp.concatenate([decay, decay], axis=0)
            qks.append(qk_kk[:c])
            a_mats.append(jnp.where(strict, beta * qk_kk[c:], 0.0))
            rhss.append(jnp.concatenate([vh * beta, kh * (beta * jnp.exp(g_c))], axis=1))
            g_cols.append(g_c)
            g_tots.append(g_t)
            where.append((s, h))
            if h % 2 == 1:
                yield

    n_invs = [-(pm_ref[0, 0] * a) for a in a_mats]
    a_bf = [a.astype(BF16) for a in a_mats]
    for lv in range(1, N_LEVELS):
        pm = pm_ref[0, lv]
        firsts = [pm * (a + jnp.dot(n.astype(BF16), ab, preferred_element_type=F32))
                  for n, a, ab in zip(n_invs, a_mats, a_bf)]
        yield
        n_invs = [n - e - _bdot(e, n) for n, e in zip(n_invs, firsts)]
        yield
    uws = [r + _bdot(n, r) for n, r in zip(n_invs, rhss)]
    yield

    for i, (s, h) in enumerate(where):
        hs = slice(h * HEAD_DIM, (h + 1) * HEAD_DIM)
        qh, kh = q_ref[s, :, hs], k_ref[s, :, hs]
        g_c, g_t = g_cols[i], g_tots[i]
        u, w = uws[i][:, :HEAD_DIM], uws[i][:, HEAD_DIM:]
        s_prev = s_scr[s, h]
        ws_qs = _bdot(jnp.concatenate([w, qh * jnp.exp(g_c)], axis=0), s_prev)
        v_new = u - ws_qs[:c]
        o_ref[0, s, :, hs] = (ws_qs[c:] + _bdot(qks[i], v_new)).astype(BF16)
        kd = kh * jnp.exp(g_t - g_c)
        s_scr[s, h] = jnp.exp(g_t) * s_prev + _bdot_tn(kd, v_new)
        yield


def _mixer_kernel(*refs, has_s0, has_out):
    it = iter(refs)
    zq_ref, zf_ref, zv_ref, lb_ref, tri3_ref, pm_ref, qr_ref = (next(it) for _ in range(7))
    gq_ref, gk_ref, gv_ref, tail_ref, par_ref = (next(it) for _ in range(5))
    s0h_ref, s0g_ref = (next(it), next(it)) if has_s0 else (None, None)
    if has_out:
        next(it), next(it)
    oh_ref, og_ref = next(it), next(it)
    sfh_ref, sfg_ref = (next(it), next(it)) if has_out else (None, None)
    sh_scr, sg_scr = next(it), next(it)
    d = pl.program_id(0)

    @pl.when(pl.program_id(2) == 0)
    def _():
        if has_s0:
            sh_scr[...] = s0h_ref[:, 0, 0]
            sg_scr[...] = s0g_ref[:, 0, 0]
        else:
            sh_scr[...] = jnp.zeros_like(sh_scr)
            sg_scr[...] = jnp.zeros_like(sg_scr)

    tri3 = tri3_ref[0]
    pending = [_gdn_stages(gq_ref, gk_ref, gv_ref, tail_ref, par_ref, tri3, pm_ref, og_ref, sg_scr, d),
               _hgrn_stages(zq_ref, zf_ref, zv_ref, lb_ref, tri3, pm_ref, qr_ref, oh_ref, sh_scr, d)]
    while pending:
        for g in list(pending):
            try:
                next(g)
            except StopIteration:
                pending.remove(g)

    if has_out:
        @pl.when(pl.program_id(2) == pl.num_programs(2) - 1)
        def _():
            sfh_ref[:, 0, 0] = sh_scr[...]
            sfg_ref[:, 0, 0] = sg_scr[...]


def _mixer_call(z, lb_logs, gdn_par, tables, s_h0, s_g0, layer, n_seq, seq, state_acc):
    has_out = state_acc is not None
    ns = SEQS_PER_STEP
    n_chunks = seq // CHUNK
    has_s0 = s_h0 is not None
    tri, pmask, qrow = tables
    tri3 = jnp.concatenate([tri, tri, tri], axis=-1).astype(BF16)
    z3 = z.reshape(n_seq, seq, D_IN_PAD)
    rows = lambda d, b, j: (b, _chunk_index(d, j, n_chunks))
    blk = lambda width, col: pl.BlockSpec((ns, CHUNK, width), lambda d, b, j: (*rows(d, b, j), col(d)))
    state_in = pl.BlockSpec((ns, 1, 1, N_HEADS, HEAD_DIM, HEAD_DIM), lambda d, b, j: (b, layer, d, 0, 0, 0))
    state_out = pl.BlockSpec((ns, 1, 1, N_HEADS, HEAD_DIM, HEAD_DIM), lambda d, b, j: (b, layer, d, 0, 0, 0))
    o_spec = pl.BlockSpec((1, ns, CHUNK, D_MODEL), lambda d, b, j: (d, *rows(d, b, j), 0))
    o_shape = jax.ShapeDtypeStruct((N_DIR, n_seq, seq, D_MODEL), BF16)
    s_shape = jax.ShapeDtypeStruct((n_seq, DEPTH, N_DIR, N_HEADS, HEAD_DIM, HEAD_DIM), F32)
    in_specs = [blk(D_MODEL, lambda d: COL_Q), blk(D_MODEL, lambda d: COL_F + d), blk(D_MODEL, lambda d: COL_I),
                pl.BlockSpec((1, 2, D_MODEL), lambda d, b, j: (d, 0, 0)),
                pl.BlockSpec((1, CHUNK, 3 * CHUNK), lambda d, b, j: (d, 0, 0)),
                pl.BlockSpec((1, N_LEVELS, CHUNK, CHUNK), lambda d, b, j: (d, 0, 0, 0)),
                pl.BlockSpec((1, N_LEVELS, CHUNK, 1), lambda d, b, j: (d, 0, 0, 0)),
                blk(D_MODEL, lambda d: COL_QKV), blk(D_MODEL, lambda d: COL_QKV + 1),
                blk(D_MODEL, lambda d: COL_QKV + 2),
                blk(128, lambda d: COL_TAIL // 128),
                pl.BlockSpec((2, 128), lambda d, b, j: (0, 0))]
    args = [z3, z3, z3, lb_logs, tri3, pmask, qrow, z3, z3, z3, z3, gdn_par]
    if has_s0:
        in_specs += [state_in, state_in]
        args += [s_h0, s_g0]
    aliases = {}
    if has_out:
        aliases = {len(args): 2, len(args) + 1: 3}
        in_specs += [pl.BlockSpec(memory_space=pl.ANY)] * 2
        args += list(state_acc)
    res = pl.pallas_call(
        functools.partial(_mixer_kernel, has_s0=has_s0, has_out=has_out),
        grid=(N_DIR, n_seq // ns, n_chunks),
        in_specs=in_specs,
        out_specs=[o_spec, o_spec] + ([state_out, state_out] if has_out else []),
        out_shape=[o_shape, o_shape] + ([s_shape, s_shape] if has_out else []),
        scratch_shapes=[pltpu.VMEM((ns, N_HEADS, HEAD_DIM, HEAD_DIM), F32)] * 2,
        input_output_aliases=aliases,
        compiler_params=_cparams(("parallel", "parallel", "arbitrary")),
        name="mixer_scan",
    )(*args)
    oh = res[0].reshape(N_DIR, n_seq * seq, D_MODEL)
    og = res[1].reshape(N_DIR, n_seq * seq, D_MODEL)
    return (oh, og, (res[2], res[3])) if has_out else (oh, og, None)


def _head_rmsnorm(x, gain):
    parts = []
    for h in range(N_HEADS):
        xh = x[:, h * HEAD_DIM:(h + 1) * HEAD_DIM]
        parts.append(xh * lax.rsqrt(jnp.mean(xh * xh, axis=-1, keepdims=True) + EPS))
    return jnp.concatenate(parts, axis=-1) * gain


def _post_kernel(oh_ref, og_ref, zoh_ref, zog_ref, zgh_ref, zgg_ref, x_ref, g1_ref, hn_ref, gn_ref,
                 wph_ref, wpg_ref, wo_ref, o_ref):
    oh = _head_rmsnorm(oh_ref[0].astype(F32) + oh_ref[1].astype(F32), hn_ref[...]) * _silu(zoh_ref[...])
    og = _head_rmsnorm(og_ref[0].astype(F32) + og_ref[1].astype(F32), gn_ref[...]) * _silu(zog_ref[...])
    y = (_sigmoid(zgh_ref[...]) * _bdot(oh, wph_ref[...])
         + _sigmoid(zgg_ref[...]) * _bdot(og, wpg_ref[...]))
    o_ref[...] = x_ref[...] + g1_ref[0] * _bdot(y, wo_ref[...])


def _post_call(oh, og, z, x, mod, hn, gn, wph, wpg, wo, rows_per_mod, tm):
    n = x.shape[0]
    mod_idx = lambda i: (i * tm) // rows_per_mod
    zspec = lambda col: pl.BlockSpec((tm, D_MODEL), lambda i, col=col: (i, col))
    wspec = pl.BlockSpec((D_MODEL, D_MODEL), lambda i: (0, 0))
    vspec = pl.BlockSpec((1, D_MODEL), lambda i: (0, 0))
    return pl.pallas_call(
        _post_kernel,
        grid=(n // tm,),
        in_specs=[pl.BlockSpec((N_DIR, tm, D_MODEL), lambda i: (0, i, 0)),
                  pl.BlockSpec((N_DIR, tm, D_MODEL), lambda i: (0, i, 0)),
                  zspec(COL_OGH), zspec(COL_OGG), zspec(COL_GATE_H), zspec(COL_GATE_G),
                  pl.BlockSpec((tm, D_MODEL), lambda i: (i, 0)),
                  pl.BlockSpec((1, 1, D_MODEL), lambda i: (mod_idx(i), 0, 2)),
                  vspec, vspec, wspec, wspec, wspec],
        out_specs=pl.BlockSpec((tm, D_MODEL), lambda i: (i, 0)),
        out_shape=jax.ShapeDtypeStruct((n, D_MODEL), F32),
        compiler_params=_cparams(("parallel",)),
        name="mix_out",
    )(oh, og, z, z, z, z, x, mod, hn, gn, wph, wpg, wo)


FFN_SUB = 256
FFN_TILE = 256


def _ffn_up_kernel(x_ref, sh_ref, sc_ref, gain_ref, wab_ref, cw_ref, cb_ref, o_ref, h_scr, *, rows, cols):
    tr = x_ref.shape[0]
    tf = o_ref.shape[1]
    sub = FFN_SUB
    n_sub = tr // sub

    @pl.when(pl.program_id(1) == 0)
    def _():
        h_scr[...] = _norm_mod(x_ref[...], gain_ref[...], sh_ref[0], sc_ref[0]).astype(BF16)

    t = lax.broadcasted_iota(jnp.int32, (sub, 1), 0)
    col = t & (cols - 1)
    first_col, last_col = col >= 1, col <= cols - 2
    w = [[cw_ref[dr, dc:dc + 1, :] for dc in range(3)] for dr in range(3)]
    bias = cb_ref[...]

    def project(i):
        ab = jnp.dot(h_scr[i * sub:(i + 1) * sub, :], wab_ref[...], preferred_element_type=F32)
        a, gate = ab[:, :tf], ab[:, tf:]
        a_l = jnp.where(first_col, pltpu.roll(a, 1, 0), 0.0)
        a_r = jnp.where(last_col, pltpu.roll(a, sub - 1, 0), 0.0)
        taps = [a_l * w[dr][0] + a * w[dr][1] + a_r * w[dr][2] for dr in ((0, 1, 2) if rows > 1 else (1,))]
        return taps, gate

    def finish(i, prev, cur, nxt):
        taps, gate = cur
        if rows > 1:
            up_src, mid, dn_src = taps
            r = ((t + i * sub) >> int(math.log2(cols))) & (rows - 1)
            zeros = jnp.zeros((cols, tf), F32)
            above = jnp.concatenate([prev[0][0][sub - cols:] if prev is not None else zeros, up_src[:sub - cols]], axis=0)
            below = jnp.concatenate([dn_src[cols:], nxt[0][2][:cols] if nxt is not None else zeros], axis=0)
            if tr > rows * cols:
                above, below = jnp.where(r >= 1, above, 0.0), jnp.where(r <= rows - 2, below, 0.0)
            conv = mid + above + below
        else:
            conv = taps[0]
        o_ref[i * sub:(i + 1) * sub, :] = (_silu(conv + bias) * gate).astype(BF16)

    blocks = [None] * n_sub
    for i in range(n_sub):
        blocks[i] = project(i)
        if i >= 1:
            finish(i - 1, blocks[i - 2] if i >= 2 else None, blocks[i - 1], blocks[i])
    finish(n_sub - 1, blocks[n_sub - 2] if n_sub >= 2 else None, blocks[n_sub - 1], None)


def _ffn_down_kernel(a_ref, x_ref, g2_ref, wo_ref, *rest, final_norm):
    if final_norm:
        nf_ref, o_ref = rest
    else:
        (o_ref,) = rest
    y = x_ref[...] + g2_ref[0] * jnp.dot(a_ref[...], wo_ref[...], preferred_element_type=F32)
    if final_norm:
        y = y * lax.rsqrt(jnp.mean(y * y, axis=-1, keepdims=True) + EPS) * nf_ref[...]
    o_ref[...] = y


def _ffn_call(x, mod, gain, w_ab, cw, cb, w_out, rows_per_mod, tr, rows, cols, norm_final):
    n = x.shape[0]
    tf = FFN_TILE
    nj = D_FF // tf
    mod_idx = lambda i: (i * tr) // rows_per_mod
    mspec = lambda k: pl.BlockSpec((1, 1, D_MODEL), lambda i, j, k=k: (mod_idx(i), 0, k))
    assert tr % FFN_SUB == 0 and FFN_SUB % cols == 0
    act = pl.pallas_call(
        functools.partial(_ffn_up_kernel, rows=rows, cols=cols),
        grid=(n // tr, nj),
        in_specs=[pl.BlockSpec((tr, D_MODEL), lambda i, j: (i, 0)),
                  mspec(3), mspec(4),
                  pl.BlockSpec((1, D_MODEL), lambda i, j: (0, 0)),
                  pl.BlockSpec((D_MODEL, 2 * tf), lambda i, j: (0, j)),
                  pl.BlockSpec((3, 3, tf), lambda i, j: (0, 0, j)),
                  pl.BlockSpec((1, tf), lambda i, j: (0, j))],
        out_specs=pl.BlockSpec((tr, tf), lambda i, j: (i, j)),
        out_shape=jax.ShapeDtypeStruct((n, D_FF), BF16),
        scratch_shapes=[pltpu.VMEM((tr, D_MODEL), BF16)],
        compiler_params=_cparams(("parallel", "arbitrary")),
        name="ffn_up",
    )(x, mod, mod, gain, w_ab, cw, cb)

    tm = min(rows_per_mod, 512)
    mod_idx2 = lambda i: (i * tm) // rows_per_mod
    final_norm = norm_final is not None
    in_specs = [pl.BlockSpec((tm, D_FF), lambda i: (i, 0)),
                pl.BlockSpec((tm, D_MODEL), lambda i: (i, 0)),
                pl.BlockSpec((1, 1, D_MODEL), lambda i: (mod_idx2(i), 0, 5)),
                pl.BlockSpec((D_FF, D_MODEL), lambda i: (0, 0))]
    args = [act, x, mod, w_out]
    if final_norm:
        in_specs.append(pl.BlockSpec((1, D_MODEL), lambda i: (0, 0)))
        args.append(norm_final)
    return pl.pallas_call(
        functools.partial(_ffn_down_kernel, final_norm=final_norm),
        grid=(n // tm,),
        in_specs=in_specs,
        out_specs=pl.BlockSpec((tm, D_MODEL), lambda i: (i, 0)),
        out_shape=jax.ShapeDtypeStruct((n, D_MODEL), F32),
        compiler_params=_cparams(("parallel",)),
        name="ffn_down",
    )(*args)


def _run_pass(x, mod_of_layer, rows_per_mod, n_seq, seq, conv_rows, conv_cols, s_h0, s_g0, want_states,
              prm, tables):
    n = n_seq * seq
    assert conv_rows & (conv_rows - 1) == 0 and conv_cols & (conv_cols - 1) == 0
    ffn_rows = min(rows_per_mod, max(seq, 2048))
    assert ffn_rows % seq == 0 and n % ffn_rows == 0
    state_shape = (n_seq, DEPTH, N_DIR, N_HEADS, HEAD_DIM, HEAD_DIM)
    states = (jnp.zeros(state_shape, F32), jnp.zeros(state_shape, F32)) if want_states else None
    for l in range(DEPTH):
        mod = mod_of_layer(l)
        z = _inproj_call(x, mod, prm["norm_mix"][l], prm["w_in"][l], prm["gdn_conv"][l], rows_per_mod, seq)
        oh, og, states = _mixer_call(z, prm["lb_logs"][l], prm["gdn_par"][l], tables, s_h0, s_g0, l,
                                     n_seq, seq, states)
        x = _post_call(oh, og, z, x, mod, prm["hgrn_norm"][l], prm["gdn_norm"][l], prm["w_proj_h"][l],
                       prm["w_proj_g"][l], prm["w_out"][l], rows_per_mod, tm=256)
        x = _ffn_call(x, mod, prm["norm_ffn"][l], prm["w_ffn_in"][l], prm["ffn_conv"][l], prm["ffn_conv_b"][l],
                      prm["w_ffn_out"][l], rows_per_mod, ffn_rows, conv_rows, conv_cols,
                      prm["norm_final"] if l == DEPTH - 1 else None)
    return x, states


def kernel(x_prompt, x_sample, c, state_hgrn, state_gdn, c_ctx, w_ada, b_ada, norm_mix, norm_ffn, w_in, hgrn_lb, hgrn_norm, gdn_conv, gdn_a_log, gdn_dt_bias, gdn_norm, w_proj_h, w_proj_g, w_out, w_ffn_in, ffn_conv, ffn_conv_b, w_ffn_out, norm_final):
    ctx_b, ctx_len, _ = x_prompt.shape
    dec_b, dec_len, _ = x_sample.shape

    lb_cum = jnp.cumsum(jax.nn.softmax(hgrn_lb.astype(F32), axis=0), axis=0)
    lower = lb_cum - lb_cum[0]
    lb_logs = jnp.stack([jnp.log(lower), jnp.log1p(-lower)], axis=2)
    hgrn_w, qkv_w = 5 * D_MODEL, 3 * D_MODEL
    small = 2 * N_DIR * N_HEADS
    tail_at = hgrn_w + qkv_w + D_MODEL
    w16 = w_in.astype(BF16)
    w_in_r = jnp.concatenate(
        [w16[..., hgrn_w:hgrn_w + qkv_w], w16[..., :hgrn_w], w16[..., hgrn_w + qkv_w:tail_at],
         w16[..., tail_at + small:], w16[..., tail_at:tail_at + small],
         jnp.zeros((DEPTH, D_MODEL, TAIL_W - small), BF16)], axis=-1)
    nj = D_FF // FFN_TILE
    w_ffn_ab = jnp.concatenate([w_ffn_in[..., :D_FF].reshape(DEPTH, D_MODEL, nj, FFN_TILE),
                                w_ffn_in[..., D_FF:].reshape(DEPTH, D_MODEL, nj, FFN_TILE)],
                               axis=-1).reshape(DEPTH, D_MODEL, 2 * D_FF).astype(BF16)
    gdn_par = jnp.zeros((DEPTH, 2, 128), F32)
    gdn_par = gdn_par.at[:, 0, TAIL_DECAY:TAIL_DECAY + N_DIR * N_HEADS].set(
        -jnp.exp(gdn_a_log.astype(F32)).reshape(DEPTH, -1))
    gdn_par = gdn_par.at[:, 1, TAIL_DECAY:TAIL_DECAY + N_DIR * N_HEADS].set(
        gdn_dt_bias.astype(F32).reshape(DEPTH, -1))
    prm = dict(
        norm_mix=norm_mix.reshape(DEPTH, 1, D_MODEL), norm_ffn=norm_ffn.reshape(DEPTH, 1, D_MODEL),
        w_in=w_in_r, lb_logs=lb_logs, gdn_conv=gdn_conv, gdn_par=gdn_par,
        hgrn_norm=jnp.tile(hgrn_norm, (1, N_HEADS)).reshape(DEPTH, 1, D_MODEL),
        gdn_norm=jnp.tile(gdn_norm, (1, N_HEADS)).reshape(DEPTH, 1, D_MODEL),
        w_proj_h=w_proj_h.astype(BF16), w_proj_g=w_proj_g.astype(BF16), w_out=w_out.astype(BF16),
        w_ffn_in=w_ffn_ab, ffn_conv=ffn_conv, ffn_conv_b=ffn_conv_b.reshape(DEPTH, 1, D_FF),
        w_ffn_out=w_ffn_out.astype(BF16), norm_final=norm_final.reshape(1, D_MODEL))
    tables = _scan_tables()

    cond = jnp.concatenate([c_ctx[None], c, jnp.zeros((16 - 1 - dec_b, D_MODEL), F32)], axis=0)
    mod = _ada_call(cond, w_ada, b_ada)

    n_ctx = ctx_b * ctx_len
    yc, (new_h, new_g) = _run_pass(x_prompt.reshape(n_ctx, D_MODEL), lambda l: mod[l, 0:1, None, :], n_ctx,
                           ctx_b, ctx_len, 1, ctx_len, None, None, True, prm, tables)
    n_dec = dec_b * dec_len
    yd, _ = _run_pass(x_sample.reshape(n_dec, D_MODEL), lambda l: mod[l, 1:1 + dec_b, None, :], dec_len,
                         dec_b, dec_len, dec_len // GRID_W, GRID_W, state_hgrn, state_gdn, False, prm, tables)
    return (yc.reshape(ctx_b, ctx_len, D_MODEL), yd.reshape(dec_b, dec_len, D_MODEL),
            new_h, new_g)
```

```python
import functools
import math

import jax
import jax.numpy as jnp
import numpy as np
from jax import lax
from jax.experimental import pallas as pl
from jax.experimental.pallas import tpu as pltpu

F32 = jnp.float32
BF16 = jnp.bfloat16

D_MODEL = 1024
DEPTH = 4
GRID_W = 64
N_HEADS = 8
HEAD_DIM = 128
D_FF = 2816
QKV_CONV = 5
N_DIR = 2
N_MOD = 6
EPS = 1e-6

CHUNK = 128
N_LEVELS = int(math.log2(CHUNK))

COL_QKV, COL_Q, COL_F, COL_I, COL_OGH, COL_OGG, COL_GATE_H, COL_GATE_G = 0, 3, 4, 6, 7, 8, 9, 10
COL_TAIL = 11 * D_MODEL
TAIL_W = 256
D_IN_PAD = COL_TAIL + TAIL_W
TAIL_BETA, TAIL_DECAY = 0, N_DIR * N_HEADS

VMEM_LIMIT_BYTES = 60 * 1024 * 1024


def _cparams(sem):
    return pltpu.CompilerParams(dimension_semantics=sem, vmem_limit_bytes=VMEM_LIMIT_BYTES)


def _bdot(a, b):
    return jnp.dot(a.astype(BF16), b.astype(BF16), preferred_element_type=F32)


def _bdot_nt(a, b):
    return lax.dot_general(a.astype(BF16), b.astype(BF16), (((1,), (1,)), ((), ())),
                           preferred_element_type=F32)


def _bdot_tn(a, b):
    return lax.dot_general(a.astype(BF16), b.astype(BF16), (((0,), (0,)), ((), ())),
                           preferred_element_type=F32)


def _sigmoid(x):
    return 1.0 / (1.0 + jnp.exp(-x))


def _silu(x):
    return x * _sigmoid(x)


def _ada_kernel(c_ref, w_ref, b_ref, o_ref):
    o_ref[0] = _bdot(_silu(c_ref[...]), w_ref[0]) + b_ref[0]


def _ada_call(cond, w_ada, b_ada):
    rows = cond.shape[0]
    tn = 1536
    return pl.pallas_call(
        _ada_kernel,
        grid=(DEPTH, N_MOD * D_MODEL // tn),
        in_specs=[pl.BlockSpec((rows, D_MODEL), lambda l, j: (0, 0)),
                  pl.BlockSpec((1, D_MODEL, tn), lambda l, j: (l, 0, j)),
                  pl.BlockSpec((1, 1, tn), lambda l, j: (l, 0, j))],
        out_specs=pl.BlockSpec((1, rows, tn), lambda l, j: (l, 0, j)),
        out_shape=jax.ShapeDtypeStruct((DEPTH, rows, N_MOD * D_MODEL), F32),
        compiler_params=_cparams(("parallel", "parallel")),
        name="ada_mod",
    )(cond, w_ada, b_ada.reshape(DEPTH, 1, N_MOD * D_MODEL))


def _norm_mod(x, gain, shift, scale):
    y = x * lax.rsqrt(jnp.mean(x * x, axis=-1, keepdims=True) + EPS)
    return (y * gain) * (1.0 + scale) + shift


INPROJ_TN = 768


def _inproj_kernel(x_ref, sh_ref, sc_ref, g_ref, w_ref, cw_ref, o_ref, h_scr, *, seq):
    j = pl.program_id(1)
    tm = x_ref.shape[0]
    conv_tiles = 3 * D_MODEL // INPROJ_TN

    @pl.when(j == 0)
    def _():
        h_scr[...] = _norm_mod(x_ref[...], g_ref[...], sh_ref[0], sc_ref[0]).astype(BF16)

    z = jnp.dot(h_scr[...], w_ref[...], preferred_element_type=F32)

    @pl.when(j >= conv_tiles)
    def _():
        o_ref[...] = z

    @pl.when(j < conv_tiles)
    def _():
        t = lax.broadcasted_iota(jnp.int32, (tm, 1), 0) & (seq - 1)
        pad = QKV_CONV // 2
        for i in range(INPROJ_TN // HEAD_DIM):
            cs = slice(i * HEAD_DIM, (i + 1) * HEAD_DIM)
            zi = z[:, cs]
            acc = zi * cw_ref[pad:pad + 1, cs]
            for s in range(1, pad + 1):
                prev = jnp.where(t >= s, pltpu.roll(zi, s, 0), 0.0)
                nxt = jnp.where(t < seq - s, pltpu.roll(zi, tm - s, 0), 0.0)
                acc = acc + prev * cw_ref[pad - s:pad - s + 1, cs] + nxt * cw_ref[pad + s:pad + s + 1, cs]
            y = _silu(acc)
            head = j * (INPROJ_TN // HEAD_DIM) + i
            scale = jnp.where(head < N_HEADS, HEAD_DIM ** -0.5, 1.0)
            normed = y * (lax.rsqrt(jnp.sum(y * y, axis=-1, keepdims=True) + EPS) * scale)
            o_ref[:, cs] = jnp.where(head >= 2 * N_HEADS, y, normed)


def _inproj_call(x, mod, gain, w, conv_w, rows_per_mod, seq):
    n = x.shape[0]
    tm = min(rows_per_mod, 2048)
    tn = INPROJ_TN
    assert tm % seq == 0 and n % tm == 0 and seq & (seq - 1) == 0
    conv_tiles = 3 * D_MODEL // tn
    mod_idx = lambda i: (i * tm) // rows_per_mod
    return pl.pallas_call(
        functools.partial(_inproj_kernel, seq=seq),
        grid=(n // tm, D_IN_PAD // tn),
        in_specs=[pl.BlockSpec((tm, D_MODEL), lambda i, j: (i, 0)),
                  pl.BlockSpec((1, 1, D_MODEL), lambda i, j: (mod_idx(i), 0, 0)),
                  pl.BlockSpec((1, 1, D_MODEL), lambda i, j: (mod_idx(i), 0, 1)),
                  pl.BlockSpec((1, D_MODEL), lambda i, j: (0, 0)),
                  pl.BlockSpec((D_MODEL, tn), lambda i, j: (0, j)),
                  pl.BlockSpec((QKV_CONV, tn), lambda i, j: (0, jnp.minimum(j, conv_tiles - 1)))],
        out_specs=pl.BlockSpec((tm, tn), lambda i, j: (i, j)),
        out_shape=jax.ShapeDtypeStruct((n, D_IN_PAD), F32),
        scratch_shapes=[pltpu.VMEM((tm, D_MODEL), BF16)],
        compiler_params=_cparams(("parallel", "arbitrary")),
        name="in_proj",
    )(x, mod, mod, gain, w, conv_w)


def _scan_tables():
    c = CHUNK
    idx = np.arange(c)
    tri = np.zeros((N_DIR, c, c), np.float32)
    tri[0] = (idx[None, :] <= idx[:, None])
    tri[1] = (idx[None, :] >= idx[:, None])
    pmask = np.zeros((N_DIR, N_LEVELS, c, c), np.float32)
    qrow = np.zeros((N_DIR, N_LEVELS, c, 1), np.float32)
    for lv in range(N_LEVELS):
        m = 1 << lv
        blk = idx // (2 * m)
        upper = (idx // m) % 2 == 1
        same = blk[:, None] == blk[None, :]
        pmask[0, lv] = same & upper[:, None] & ~upper[None, :]
        qrow[0, lv, :, 0] = upper
        pmask[1, lv] = same & ~upper[:, None] & upper[None, :]
        qrow[1, lv, :, 0] = ~upper
    return jnp.asarray(tri), jnp.asarray(pmask), jnp.asarray(qrow)


def _split3(x):
    hi = x.astype(BF16)
    r1 = x - hi.astype(F32)
    mid = r1.astype(BF16)
    lo = (r1 - mid.astype(F32)).astype(BF16)
    return hi, mid, lo


def _dot_table(table3, x):
    return jnp.dot(table3, jnp.concatenate(_split3(x), axis=0), preferred_element_type=F32)


def _level_reference_rows(b, lv, d):
    c, width = b.shape
    m = 1 << lv
    if 2 * m >= 8:
        b3 = b.reshape(c // (2 * m), 2 * m, width)
        ref = jnp.where(d == 0, b3[:, m - 1:m, :], b3[:, m:m + 1, :])
        return jnp.broadcast_to(ref, b3.shape).reshape(c, width)
    assert m == 2
    b3 = b.reshape(c // 8, 8, width)
    first = jnp.where(d == 0, b3[:, 1:2, :], b3[:, 2:3, :])
    second = jnp.where(d == 0, b3[:, 5:6, :], b3[:, 6:7, :])
    r = lax.broadcasted_iota(jnp.int32, (1, 8, 1), 1)
    return jnp.where(r < 4, first, second).reshape(c, width)


def _chunk_index(d, j, n_chunks):
    return j + d * (n_chunks - 1 - 2 * j)


SEQS_PER_STEP = 2


def _hgrn_stages(q_ref, f_ref, v_ref, lb_ref, tri3, pm_ref, qr_ref, o_ref, s_scr, d):
    for s in range(SEQS_PER_STEP):
        x = f_ref[s]
        logsig = jnp.minimum(x, 0.0) - jnp.log(1.0 + jnp.exp(-jnp.abs(x)))
        la = lb_ref[0, 0:1, :]
        lc = lb_ref[0, 1:2, :] + logsig
        logf = jnp.maximum(la, lc) + jnp.log(1.0 + jnp.exp(-jnp.abs(la - lc)))
        kf = jnp.exp(lc - x)
        q = _silu(q_ref[s]) * HEAD_DIM ** -0.5
        v = v_ref[s]
        yield
        b = _dot_table(tri3, logf)
        btot = jnp.sum(logf, axis=0, keepdims=True)
        xs = [jnp.where(qr_ref[0, 0] > 0.5, q * jnp.exp(logf), kf)]
        yield
        for lv in range(1, N_LEVELS):
            e = jnp.exp(-jnp.abs(b - _level_reference_rows(b, lv, d)))
            xs.append(jnp.where(qr_ref[0, lv] > 0.5, q, kf) * e)
            yield
        q_in = q * jnp.exp(b)
        k_out = kf * jnp.exp(btot - b)
        e_tot = jnp.exp(btot)
        yield
        for h in range(N_HEADS):
            hs = slice(h * HEAD_DIM, (h + 1) * HEAD_DIM)
            qh, kh, vh = q[:, hs], kf[:, hs], v[:, hs]
            scores = None
            for lv in range(N_LEVELS):
                xh = xs[lv][:, hs].astype(BF16)
                part = lax.dot_general(xh, xh, (((1,), (1,)), ((), ())), preferred_element_type=F32)
                scores = jnp.where(pm_ref[0, lv] > 0.5, part, 0.0 if scores is None else scores)
            s_prev = s_scr[s, h]
            diag = jnp.sum(qh * kh, axis=-1, keepdims=True)
            o_ref[0, s, :, hs] = (_bdot(scores, vh) + diag * vh + _bdot(q_in[:, hs], s_prev)).astype(BF16)
            e_col = jnp.transpose(jnp.broadcast_to(e_tot[:, hs], (HEAD_DIM, HEAD_DIM)))[:, 0:1]
            s_scr[s, h] = e_col * s_prev + _bdot_tn(k_out[:, hs], vh)
            yield


def _gdn_stages(q_ref, k_ref, v_ref, tail_ref, par_ref, tri3, pm_ref, o_ref, s_scr, d):
    c = CHUNK
    incl = tri3[:, :c].astype(F32) > 0.5
    eye = (lax.broadcasted_iota(jnp.int32, (c, c), 0) == lax.broadcasted_iota(jnp.int32, (c, c), 1))
    strict = jnp.logical_and(incl, jnp.logical_not(eye))

    def pick(arr_fn, base, h):
        return jnp.where(d == 0, arr_fn(base + h), arr_fn(base + N_HEADS + h))

    a_mats, qks, rhss, g_cols, g_tots, where = [], [], [], [], [], []
    for s in range(SEQS_PER_STEP):
        tail = tail_ref[s]
        beta_all = _sigmoid(tail)
        xg = tail + par_ref[1:2, :]
        softplus = jnp.maximum(xg, 0.0) + jnp.log1p(jnp.exp(-jnp.abs(xg)))
        g_all = par_ref[0:1, :] * softplus
        gcum = _dot_table(tri3, g_all)
        gcum_t = jnp.transpose(gcum)
        gtot = jnp.sum(g_all, axis=0, keepdims=True)
        for h in range(N_HEADS):
            hs = slice(h * HEAD_DIM, (h + 1) * HEAD_DIM)
            qh, kh, vh = q_ref[s, :, hs], k_ref[s, :, hs], v_ref[s, :, hs]
            beta = pick(lambda i: beta_all[:, i:i + 1], TAIL_BETA, h)
            g_c = pick(lambda i: gcum[:, i:i + 1], TAIL_DECAY, h)
            g_r = pick(lambda i: gcum_t[i:i + 1, :], TAIL_DECAY, h)
            g_t = pick(lambda i: gtot[:, i:i + 1], TAIL_DECAY, h)
            decay = jnp.where(incl, jnp.exp(jnp.where(incl, g_c - g_r, 0.0)), 0.0)
            qk_kk = _bdot_nt(jnp.concatenate([qh, kh], axis=0), kh) * jnp.concatenate([decay, decay], axis=0)
            qks.append(qk_kk[:c])
            a_mats.append(jnp.where(strict, beta * qk_kk[c:], 0.0))
            rhss.append(jnp.concatenate([vh * beta, kh * (beta * jnp.exp(g_c))], axis=1))
            g_cols.append(g_c)
            g_tots.append(g_t)
            where.append((s, h))
            if h % 2 == 1:
                yield

    n_invs = [-(pm_ref[0, 0] * a) for a in a_mats]
    a_bf = [a.astype(BF16) for a in a_mats]
    for lv in range(1, N_LEVELS):
        pm = pm_ref[0, lv]
        firsts = [pm * (a + jnp.dot(n.astype(BF16), ab, preferred_element_type=F32))
                  for n, a, ab in zip(n_invs, a_mats, a_bf)]
        yield
        n_invs = [n - e - _bdot(e, n) for n, e in zip(n_invs, firsts)]
        yield
    uws = [r + _bdot(n, r) for n, r in zip(n_invs, rhss)]
    yield

    lefts, o_adds, s_adds = [], [], []
    for i, (s, h) in enumerate(where):
        hs = slice(h * HEAD_DIM, (h + 1) * HEAD_DIM)
        qh, kh = q_ref[s, :, hs], k_ref[s, :, hs]
        g_c, g_t = g_cols[i], g_tots[i]
        qk_uw = _bdot(qks[i], uws[i])
        kd_uw = _bdot_tn(kh * jnp.exp(g_t - g_c), uws[i])
        lefts.append(jnp.concatenate([qh * jnp.exp(g_c) - qk_uw[:, HEAD_DIM:], kd_uw[:, HEAD_DIM:]], axis=0))
        o_adds.append(qk_uw[:, :HEAD_DIM])
        s_adds.append(kd_uw[:, :HEAD_DIM])
        if i % 2 == 1:
            yield

    for i, (s, h) in enumerate(where):
        hs = slice(h * HEAD_DIM, (h + 1) * HEAD_DIM)
        s_prev = s_scr[s, h]
        both = _bdot(lefts[i], s_prev)
        o_ref[0, s, :, hs] = (both[:c] + o_adds[i]).astype(BF16)
        s_scr[s, h] = jnp.exp(g_tots[i]) * s_prev - both[c:] + s_adds[i]
        if i % 2 == 1:
            yield


def _mixer_kernel(*refs, has_s0, has_out):
    it = iter(refs)
    zq_ref, zf_ref, zv_ref, lb_ref, tri3_ref, pm_ref, qr_ref = (next(it) for _ in range(7))
    gq_ref, gk_ref, gv_ref, tail_ref, par_ref = (next(it) for _ in range(5))
    s0h_ref, s0g_ref = (next(it), next(it)) if has_s0 else (None, None)
    if has_out:
        next(it), next(it)
    oh_ref, og_ref = next(it), next(it)
    sfh_ref, sfg_ref = (next(it), next(it)) if has_out else (None, None)
    sh_scr, sg_scr = next(it), next(it)
    d = pl.program_id(0)

    @pl.when(pl.program_id(2) == 0)
    def _():
        if has_s0:
            sh_scr[...] = s0h_ref[:, 0, 0]
            sg_scr[...] = s0g_ref[:, 0, 0]
        else:
            sh_scr[...] = jnp.zeros_like(sh_scr)
            sg_scr[...] = jnp.zeros_like(sg_scr)

    tri3 = tri3_ref[0]
    pending = [_gdn_stages(gq_ref, gk_ref, gv_ref, tail_ref, par_ref, tri3, pm_ref, og_ref, sg_scr, d),
               _hgrn_stages(zq_ref, zf_ref, zv_ref, lb_ref, tri3, pm_ref, qr_ref, oh_ref, sh_scr, d)]
    while pending:
        for g in list(pending):
            try:
                next(g)
            except StopIteration:
                pending.remove(g)

    if has_out:
        @pl.when(pl.program_id(2) == pl.num_programs(2) - 1)
        def _():
            sfh_ref[:, 0, 0] = sh_scr[...]
            sfg_ref[:, 0, 0] = sg_scr[...]


def _mixer_call(z, lb_logs, gdn_par, tables, s_h0, s_g0, layer, n_seq, seq, state_acc):
    has_out = state_acc is not None
    ns = SEQS_PER_STEP
    n_chunks = seq // CHUNK
    has_s0 = s_h0 is not None
    tri, pmask, qrow = tables
    tri3 = jnp.concatenate([tri, tri, tri], axis=-1).astype(BF16)
    z3 = z.reshape(n_seq, seq, D_IN_PAD)
    rows = lambda d, b, j: (b, _chunk_index(d, j, n_chunks))
    blk = lambda width, col: pl.BlockSpec((ns, CHUNK, width), lambda d, b, j: (*rows(d, b, j), col(d)))
    state_in = pl.BlockSpec((ns, 1, 1, N_HEADS, HEAD_DIM, HEAD_DIM), lambda d, b, j: (b, layer, d, 0, 0, 0))
    state_out = pl.BlockSpec((ns, 1, 1, N_HEADS, HEAD_DIM, HEAD_DIM), lambda d, b, j: (b, layer, d, 0, 0, 0))
    o_spec = pl.BlockSpec((1, ns, CHUNK, D_MODEL), lambda d, b, j: (d, *rows(d, b, j), 0))
    o_shape = jax.ShapeDtypeStruct((N_DIR, n_seq, seq, D_MODEL), BF16)
    s_shape = jax.ShapeDtypeStruct((n_seq, DEPTH, N_DIR, N_HEADS, HEAD_DIM, HEAD_DIM), F32)
    in_specs = [blk(D_MODEL, lambda d: COL_Q), blk(D_MODEL, lambda d: COL_F + d), blk(D_MODEL, lambda d: COL_I),
                pl.BlockSpec((1, 2, D_MODEL), lambda d, b, j: (d, 0, 0)),
                pl.BlockSpec((1, CHUNK, 3 * CHUNK), lambda d, b, j: (d, 0, 0)),
                pl.BlockSpec((1, N_LEVELS, CHUNK, CHUNK), lambda d, b, j: (d, 0, 0, 0)),
                pl.BlockSpec((1, N_LEVELS, CHUNK, 1), lambda d, b, j: (d, 0, 0, 0)),
                blk(D_MODEL, lambda d: COL_QKV), blk(D_MODEL, lambda d: COL_QKV + 1),
                blk(D_MODEL, lambda d: COL_QKV + 2),
                blk(128, lambda d: COL_TAIL // 128),
                pl.BlockSpec((2, 128), lambda d, b, j: (0, 0))]
    args = [z3, z3, z3, lb_logs, tri3, pmask, qrow, z3, z3, z3, z3, gdn_par]
    if has_s0:
        in_specs += [state_in, state_in]
        args += [s_h0, s_g0]
    aliases = {}
    if has_out:
        aliases = {len(args): 2, len(args) + 1: 3}
        in_specs += [pl.BlockSpec(memory_space=pl.ANY)] * 2
        args += list(state_acc)
    res = pl.pallas_call(
        functools.partial(_mixer_kernel, has_s0=has_s0, has_out=has_out),
        grid=(N_DIR, n_seq // ns, n_chunks),
        in_specs=in_specs,
        out_specs=[o_spec, o_spec] + ([state_out, state_out] if has_out else []),
        out_shape=[o_shape, o_shape] + ([s_shape, s_shape] if has_out else []),
        scratch_shapes=[pltpu.VMEM((ns, N_HEADS, HEAD_DIM, HEAD_DIM), F32)] * 2,
        input_output_aliases=aliases,
        compiler_params=_cparams(("parallel", "parallel", "arbitrary")),
        name="mixer_scan",
    )(*args)
    oh = res[0].reshape(N_DIR, n_seq * seq, D_MODEL)
    og = res[1].reshape(N_DIR, n_seq * seq, D_MODEL)
    return (oh, og, (res[2], res[3])) if has_out else (oh, og, None)


def _head_rmsnorm(x, gain):
    parts = []
    for h in range(N_HEADS):
        xh = x[:, h * HEAD_DIM:(h + 1) * HEAD_DIM]
        parts.append(xh * lax.rsqrt(jnp.mean(xh * xh, axis=-1, keepdims=True) + EPS))
    return jnp.concatenate(parts, axis=-1) * gain


def _post_kernel(oh_ref, og_ref, zoh_ref, zog_ref, zgh_ref, zgg_ref, x_ref, g1_ref, hn_ref, gn_ref,
                 wph_ref, wpg_ref, wo_ref, o_ref):
    oh = _head_rmsnorm(oh_ref[0].astype(F32) + oh_ref[1].astype(F32), hn_ref[...]) * _silu(zoh_ref[...])
    og = _head_rmsnorm(og_ref[0].astype(F32) + og_ref[1].astype(F32), gn_ref[...]) * _silu(zog_ref[...])
    y = (_sigmoid(zgh_ref[...]) * _bdot(oh, wph_ref[...])
         + _sigmoid(zgg_ref[...]) * _bdot(og, wpg_ref[...]))
    o_ref[...] = x_ref[...] + g1_ref[0] * _bdot(y, wo_ref[...])


def _post_call(oh, og, z, x, mod, hn, gn, wph, wpg, wo, rows_per_mod, tm):
    n = x.shape[0]
    mod_idx = lambda i: (i * tm) // rows_per_mod
    zspec = lambda col: pl.BlockSpec((tm, D_MODEL), lambda i, col=col: (i, col))
    wspec = pl.BlockSpec((D_MODEL, D_MODEL), lambda i: (0, 0))
    vspec = pl.BlockSpec((1, D_MODEL), lambda i: (0, 0))
    return pl.pallas_call(
        _post_kernel,
        grid=(n // tm,),
        in_specs=[pl.BlockSpec((N_DIR, tm, D_MODEL), lambda i: (0, i, 0)),
                  pl.BlockSpec((N_DIR, tm, D_MODEL), lambda i: (0, i, 0)),
                  zspec(COL_OGH), zspec(COL_OGG), zspec(COL_GATE_H), zspec(COL_GATE_G),
                  pl.BlockSpec((tm, D_MODEL), lambda i: (i, 0)),
                  pl.BlockSpec((1, 1, D_MODEL), lambda i: (mod_idx(i), 0, 2)),
                  vspec, vspec, wspec, wspec, wspec],
        out_specs=pl.BlockSpec((tm, D_MODEL), lambda i: (i, 0)),
        out_shape=jax.ShapeDtypeStruct((n, D_MODEL), F32),
        compiler_params=_cparams(("parallel",)),
        name="mix_out",
    )(oh, og, z, z, z, z, x, mod, hn, gn, wph, wpg, wo)


FFN_SUB = 256
FFN_TILE = 256


def _ffn_up_kernel(x_ref, sh_ref, sc_ref, gain_ref, wab_ref, cw_ref, cb_ref, o_ref, h_scr, *, rows, cols):
    tr = x_ref.shape[0]
    tf = o_ref.shape[1]
    sub = FFN_SUB
    n_sub = tr // sub

    @pl.when(pl.program_id(1) == 0)
    def _():
        h_scr[...] = _norm_mod(x_ref[...], gain_ref[...], sh_ref[0], sc_ref[0]).astype(BF16)

    t = lax.broadcasted_iota(jnp.int32, (sub, 1), 0)
    col = t & (cols - 1)
    first_col, last_col = col >= 1, col <= cols - 2
    w = [[cw_ref[dr, dc:dc + 1, :] for dc in range(3)] for dr in range(3)]
    bias = cb_ref[...]

    def project(i):
        ab = jnp.dot(h_scr[i * sub:(i + 1) * sub, :], wab_ref[...], preferred_element_type=F32)
        a, gate = ab[:, :tf], ab[:, tf:]
        a_l = jnp.where(first_col, pltpu.roll(a, 1, 0), 0.0)
        a_r = jnp.where(last_col, pltpu.roll(a, sub - 1, 0), 0.0)
        taps = [a_l * w[dr][0] + a * w[dr][1] + a_r * w[dr][2] for dr in ((0, 1, 2) if rows > 1 else (1,))]
        return taps, gate

    def finish(i, prev, cur, nxt):
        taps, gate = cur
        if rows > 1:
            up_src, mid, dn_src = taps
            r = ((t + i * sub) >> int(math.log2(cols))) & (rows - 1)
            zeros = jnp.zeros((cols, tf), F32)
            above = jnp.concatenate([prev[0][0][sub - cols:] if prev is not None else zeros, up_src[:sub - cols]], axis=0)
            below = jnp.concatenate([dn_src[cols:], nxt[0][2][:cols] if nxt is not None else zeros], axis=0)
            if tr > rows * cols:
                above, below = jnp.where(r >= 1, above, 0.0), jnp.where(r <= rows - 2, below, 0.0)
            conv = mid + above + below
        else:
            conv = taps[0]
        o_ref[i * sub:(i + 1) * sub, :] = (_silu(conv + bias) * gate).astype(BF16)

    blocks = [None] * n_sub
    for i in range(n_sub):
        blocks[i] = project(i)
        if i >= 1:
            finish(i - 1, blocks[i - 2] if i >= 2 else None, blocks[i - 1], blocks[i])
    finish(n_sub - 1, blocks[n_sub - 2] if n_sub >= 2 else None, blocks[n_sub - 1], None)


def _ffn_down_kernel(a_ref, x_ref, g2_ref, wo_ref, *rest, final_norm):
    if final_norm:
        nf_ref, o_ref = rest
    else:
        (o_ref,) = rest
    y = x_ref[...] + g2_ref[0] * jnp.dot(a_ref[...], wo_ref[...], preferred_element_type=F32)
    if final_norm:
        y = y * lax.rsqrt(jnp.mean(y * y, axis=-1, keepdims=True) + EPS) * nf_ref[...]
    o_ref[...] = y


def _ffn_call(x, mod, gain, w_ab, cw, cb, w_out, rows_per_mod, tr, rows, cols, norm_final):
    n = x.shape[0]
    tf = FFN_TILE
    nj = D_FF // tf
    mod_idx = lambda i: (i * tr) // rows_per_mod
    mspec = lambda k: pl.BlockSpec((1, 1, D_MODEL), lambda i, j, k=k: (mod_idx(i), 0, k))
    assert tr % FFN_SUB == 0 and FFN_SUB % cols == 0
    act = pl.pallas_call(
        functools.partial(_ffn_up_kernel, rows=rows, cols=cols),
        grid=(n // tr, nj),
        in_specs=[pl.BlockSpec((tr, D_MODEL), lambda i, j: (i, 0)),
                  mspec(3), mspec(4),
                  pl.BlockSpec((1, D_MODEL), lambda i, j: (0, 0)),
                  pl.BlockSpec((D_MODEL, 2 * tf), lambda i, j: (0, j)),
                  pl.BlockSpec((3, 3, tf), lambda i, j: (0, 0, j)),
                  pl.BlockSpec((1, tf), lambda i, j: (0, j))],
        out_specs=pl.BlockSpec((tr, tf), lambda i, j: (i, j)),
        out_shape=jax.ShapeDtypeStruct((n, D_FF), BF16),
        scratch_shapes=[pltpu.VMEM((tr, D_MODEL), BF16)],
        compiler_params=_cparams(("parallel", "arbitrary")),
        name="ffn_up",
    )(x, mod, mod, gain, w_ab, cw, cb)

    tm = min(rows_per_mod, 512)
    mod_idx2 = lambda i: (i * tm) // rows_per_mod
    final_norm = norm_final is not None
    in_specs = [pl.BlockSpec((tm, D_FF), lambda i: (i, 0)),
                pl.BlockSpec((tm, D_MODEL), lambda i: (i, 0)),
                pl.BlockSpec((1, 1, D_MODEL), lambda i: (mod_idx2(i), 0, 5)),
                pl.BlockSpec((D_FF, D_MODEL), lambda i: (0, 0))]
    args = [act, x, mod, w_out]
    if final_norm:
        in_specs.append(pl.BlockSpec((1, D_MODEL), lambda i: (0, 0)))
        args.append(norm_final)
    return pl.pallas_call(
        functools.partial(_ffn_down_kernel, final_norm=final_norm),
        grid=(n // tm,),
        in_specs=in_specs,
        out_specs=pl.BlockSpec((tm, D_MODEL), lambda i: (i, 0)),
        out_shape=jax.ShapeDtypeStruct((n, D_MODEL), F32),
        compiler_params=_cparams(("parallel",)),
        name="ffn_down",
    )(*args)


def _run_pass(x, mod_of_layer, rows_per_mod, n_seq, seq, conv_rows, conv_cols, s_h0, s_g0, want_states,
              prm, tables):
    n = n_seq * seq
    assert conv_rows & (conv_rows - 1) == 0 and conv_cols & (conv_cols - 1) == 0
    ffn_rows = min(rows_per_mod, max(seq, 2048))
    assert ffn_rows % seq == 0 and n % ffn_rows == 0
    state_shape = (n_seq, DEPTH, N_DIR, N_HEADS, HEAD_DIM, HEAD_DIM)
    states = (jnp.zeros(state_shape, F32), jnp.zeros(state_shape, F32)) if want_states else None
    for l in range(DEPTH):
        mod = mod_of_layer(l)
        z = _inproj_call(x, mod, prm["norm_mix"][l], prm["w_in"][l], prm["gdn_conv"][l], rows_per_mod, seq)
        oh, og, states = _mixer_call(z, prm["lb_logs"][l], prm["gdn_par"][l], tables, s_h0, s_g0, l,
                                     n_seq, seq, states)
        x = _post_call(oh, og, z, x, mod, prm["hgrn_norm"][l], prm["gdn_norm"][l], prm["w_proj_h"][l],
                       prm["w_proj_g"][l], prm["w_out"][l], rows_per_mod, tm=256)
        x = _ffn_call(x, mod, prm["norm_ffn"][l], prm["w_ffn_in"][l], prm["ffn_conv"][l], prm["ffn_conv_b"][l],
                      prm["w_ffn_out"][l], rows_per_mod, ffn_rows, conv_rows, conv_cols,
                      prm["norm_final"] if l == DEPTH - 1 else None)
    return x, states


def kernel(x_prompt, x_sample, c, state_hgrn, state_gdn, c_ctx, w_ada, b_ada, norm_mix, norm_ffn, w_in, hgrn_lb, hgrn_norm, gdn_conv, gdn_a_log, gdn_dt_bias, gdn_norm, w_proj_h, w_proj_g, w_out, w_ffn_in, ffn_conv, ffn_conv_b, w_ffn_out, norm_final):
    ctx_b, ctx_len, _ = x_prompt.shape
    dec_b, dec_len, _ = x_sample.shape

    lb_cum = jnp.cumsum(jax.nn.softmax(hgrn_lb.astype(F32), axis=0), axis=0)
    lower = lb_cum - lb_cum[0]
    lb_logs = jnp.stack([jnp.log(lower), jnp.log1p(-lower)], axis=2)
    hgrn_w, qkv_w = 5 * D_MODEL, 3 * D_MODEL
    small = 2 * N_DIR * N_HEADS
    tail_at = hgrn_w + qkv_w + D_MODEL
    w16 = w_in.astype(BF16)
    w_in_r = jnp.concatenate(
        [w16[..., hgrn_w:hgrn_w + qkv_w], w16[..., :hgrn_w], w16[..., hgrn_w + qkv_w:tail_at],
         w16[..., tail_at + small:], w16[..., tail_at:tail_at + small],
         jnp.zeros((DEPTH, D_MODEL, TAIL_W - small), BF16)], axis=-1)
    nj = D_FF // FFN_TILE
    w_ffn_ab = jnp.concatenate([w_ffn_in[..., :D_FF].reshape(DEPTH, D_MODEL, nj, FFN_TILE),
                                w_ffn_in[..., D_FF:].reshape(DEPTH, D_MODEL, nj, FFN_TILE)],
                               axis=-1).reshape(DEPTH, D_MODEL, 2 * D_FF).astype(BF16)
    gdn_par = jnp.zeros((DEPTH, 2, 128), F32)
    gdn_par = gdn_par.at[:, 0, TAIL_DECAY:TAIL_DECAY + N_DIR * N_HEADS].set(
        -jnp.exp(gdn_a_log.astype(F32)).reshape(DEPTH, -1))
    gdn_par = gdn_par.at[:, 1, TAIL_DECAY:TAIL_DECAY + N_DIR * N_HEADS].set(
        gdn_dt_bias.astype(F32).reshape(DEPTH, -1))
    prm = dict(
        norm_mix=norm_mix.reshape(DEPTH, 1, D_MODEL), norm_ffn=norm_ffn.reshape(DEPTH, 1, D_MODEL),
        w_in=w_in_r, lb_logs=lb_logs, gdn_conv=gdn_conv, gdn_par=gdn_par,
        hgrn_norm=jnp.tile(hgrn_norm, (1, N_HEADS)).reshape(DEPTH, 1, D_MODEL),
        gdn_norm=jnp.tile(gdn_norm, (1, N_HEADS)).reshape(DEPTH, 1, D_MODEL),
        w_proj_h=w_proj_h.astype(BF16), w_proj_g=w_proj_g.astype(BF16), w_out=w_out.astype(BF16),
        w_ffn_in=w_ffn_ab, ffn_conv=ffn_conv, ffn_conv_b=ffn_conv_b.reshape(DEPTH, 1, D_FF),
        w_ffn_out=w_ffn_out.astype(BF16), norm_final=norm_final.reshape(1, D_MODEL))
    tables = _scan_tables()

    cond = jnp.concatenate([c_ctx[None], c, jnp.zeros((16 - 1 - dec_b, D_MODEL), F32)], axis=0)
    mod = _ada_call(cond, w_ada, b_ada)

    n_ctx = ctx_b * ctx_len
    yc, (new_h, new_g) = _run_pass(x_prompt.reshape(n_ctx, D_MODEL), lambda l: mod[l, 0:1, None, :], n_ctx,
                           ctx_b, ctx_len, 1, ctx_len, None, None, True, prm, tables)
    n_dec = dec_b * dec_len
    yd, _ = _run_pass(x_sample.reshape(n_dec, D_MODEL), lambda l: mod[l, 1:1 + dec_b, None, :], dec_len,
                         dec_b, dec_len, dec_len // GRID_W, GRID_W, state_hgrn, state_gdn, False, prm, tables)
    return (yc.reshape(ctx_b, ctx_len, D_MODEL), yd.reshape(dec_b, dec_len, D_MODEL),
            new_h, new_g)
```

```python
import functools
import math

import jax
import jax.numpy as jnp
import numpy as np
from jax import lax
from jax.experimental import pallas as pl
from jax.experimental.pallas import tpu as pltpu

F32 = jnp.float32
BF16 = jnp.bfloat16

D_MODEL = 1024
DEPTH = 4
GRID_W = 64
N_HEADS = 8
HEAD_DIM = 128
D_FF = 2816
QKV_CONV = 5
N_DIR = 2
N_MOD = 6
EPS = 1e-6

CHUNK = 128
N_LEVELS = int(math.log2(CHUNK))

COL_QKV, COL_Q, COL_F, COL_I, COL_OGH, COL_OGG, COL_GATE_H, COL_GATE_G = 0, 3, 4, 6, 7, 8, 9, 10
COL_TAIL = 11 * D_MODEL
TAIL_W = 256
D_IN_PAD = COL_TAIL + TAIL_W
TAIL_BETA, TAIL_DECAY = 0, N_DIR * N_HEADS

VMEM_LIMIT_BYTES = 60 * 1024 * 1024


def _cparams(sem):
    return pltpu.CompilerParams(dimension_semantics=sem, vmem_limit_bytes=VMEM_LIMIT_BYTES)


def _bdot(a, b):
    return jnp.dot(a.astype(BF16), b.astype(BF16), preferred_element_type=F32)


def _bdot_nt(a, b):
    return lax.dot_general(a.astype(BF16), b.astype(BF16), (((1,), (1,)), ((), ())),
                           preferred_element_type=F32)


def _bdot_tn(a, b):
    return lax.dot_general(a.astype(BF16), b.astype(BF16), (((0,), (0,)), ((), ())),
                           preferred_element_type=F32)


def _sigmoid(x):
    return 1.0 / (1.0 + jnp.exp(-x))


def _silu(x):
    return x * _sigmoid(x)


def _ada_kernel(c_ref, w_ref, b_ref, o_ref):
    o_ref[0] = _bdot(_silu(c_ref[...]), w_ref[0]) + b_ref[0]


def _ada_call(cond, w_ada, b_ada):
    rows = cond.shape[0]
    tn = 1536
    return pl.pallas_call(
        _ada_kernel,
        grid=(DEPTH, N_MOD * D_MODEL // tn),
        in_specs=[pl.BlockSpec((rows, D_MODEL), lambda l, j: (0, 0)),
                  pl.BlockSpec((1, D_MODEL, tn), lambda l, j: (l, 0, j)),
                  pl.BlockSpec((1, 1, tn), lambda l, j: (l, 0, j))],
        out_specs=pl.BlockSpec((1, rows, tn), lambda l, j: (l, 0, j)),
        out_shape=jax.ShapeDtypeStruct((DEPTH, rows, N_MOD * D_MODEL), F32),
        compiler_params=_cparams(("parallel", "parallel")),
        name="ada_mod",
    )(cond, w_ada, b_ada.reshape(DEPTH, 1, N_MOD * D_MODEL))


def _norm_mod(x, gain, shift, scale):
    y = x * lax.rsqrt(jnp.mean(x * x, axis=-1, keepdims=True) + EPS)
    return (y * gain) * (1.0 + scale) + shift


INPROJ_TN = 768


def _inproj_kernel(x_ref, sh_ref, sc_ref, g_ref, w_ref, wt_ref, cw_ref, o_ref, h_scr, *, seq):
    j = pl.program_id(1)
    tm = x_ref.shape[0]
    conv_tiles = 3 * D_MODEL // INPROJ_TN
    main_tiles = COL_GATE_H * D_MODEL // INPROJ_TN

    @pl.when(j == 0)
    def _():
        h_scr[...] = _norm_mod(x_ref[...], g_ref[...], sh_ref[0], sc_ref[0]).astype(BF16)

    @pl.when(jnp.logical_and(j >= conv_tiles, j < main_tiles))
    def _():
        o_ref[...] = jnp.dot(h_scr[...], w_ref[...], preferred_element_type=F32)

    @pl.when(j >= main_tiles)
    def _():
        o_ref[...] = jnp.dot(h_scr[...], wt_ref[...], preferred_element_type=F32)

    @pl.when(j < conv_tiles)
    def _():
        z = jnp.dot(h_scr[...], w_ref[...], preferred_element_type=F32)
        t = lax.broadcasted_iota(jnp.int32, (tm, 1), 0) & (seq - 1)
        pad = QKV_CONV // 2
        for i in range(INPROJ_TN // HEAD_DIM):
            cs = slice(i * HEAD_DIM, (i + 1) * HEAD_DIM)
            zi = z[:, cs]
            acc = zi * cw_ref[pad:pad + 1, cs]
            for s in range(1, pad + 1):
                prev = jnp.where(t >= s, pltpu.roll(zi, s, 0), 0.0)
                nxt = jnp.where(t < seq - s, pltpu.roll(zi, tm - s, 0), 0.0)
                acc = acc + prev * cw_ref[pad - s:pad - s + 1, cs] + nxt * cw_ref[pad + s:pad + s + 1, cs]
            y = _silu(acc)
            head = j * (INPROJ_TN // HEAD_DIM) + i
            scale = jnp.where(head < N_HEADS, HEAD_DIM ** -0.5, 1.0)
            normed = y * (lax.rsqrt(jnp.sum(y * y, axis=-1, keepdims=True) + EPS) * scale)
            o_ref[:, cs] = jnp.where(head >= 2 * N_HEADS, y, normed)


def _inproj_call(x, mod, gain, w, w_tail, conv_w, rows_per_mod, seq):
    n = x.shape[0]
    tm = min(rows_per_mod, 2048)
    tn = INPROJ_TN
    assert tm % seq == 0 and n % tm == 0 and seq & (seq - 1) == 0
    conv_tiles = 3 * D_MODEL // tn
    main_tiles = w.shape[1] // tn
    tail_tiles = w_tail.shape[1] // tn
    assert (main_tiles + tail_tiles) * tn == D_IN_PAD and main_tiles * tn == COL_GATE_H * D_MODEL
    mod_idx = lambda i: (i * tm) // rows_per_mod
    return pl.pallas_call(
        functools.partial(_inproj_kernel, seq=seq),
        grid=(n // tm, D_IN_PAD // tn),
        in_specs=[pl.BlockSpec((tm, D_MODEL), lambda i, j: (i, 0)),
                  pl.BlockSpec((1, 1, D_MODEL), lambda i, j: (mod_idx(i), 0, 0)),
                  pl.BlockSpec((1, 1, D_MODEL), lambda i, j: (mod_idx(i), 0, 1)),
                  pl.BlockSpec((1, D_MODEL), lambda i, j: (0, 0)),
                  pl.BlockSpec((D_MODEL, tn), lambda i, j: (0, jnp.minimum(j, main_tiles - 1))),
                  pl.BlockSpec((D_MODEL, tn), lambda i, j: (0, jnp.clip(j - main_tiles, 0, tail_tiles - 1))),
                  pl.BlockSpec((QKV_CONV, tn), lambda i, j: (0, jnp.minimum(j, conv_tiles - 1)))],
        out_specs=pl.BlockSpec((tm, tn), lambda i, j: (i, j)),
        out_shape=jax.ShapeDtypeStruct((n, D_IN_PAD), F32),
        scratch_shapes=[pltpu.VMEM((tm, D_MODEL), BF16)],
        compiler_params=_cparams(("parallel", "arbitrary")),
        name="in_proj",
    )(x, mod, mod, gain, w, w_tail, conv_w)


def _relayout_kernel(src_ref, o_ref):
    o_ref[...] = src_ref[...].astype(BF16)


def _w_in_main(w_in):
    n_hgrn = COL_OGG - COL_Q
    n_main = COL_GATE_H
    src_block = lambda c: jnp.where(c < COL_Q, c + n_hgrn, jnp.where(c < COL_OGG, c - COL_Q, c))
    return pl.pallas_call(
        _relayout_kernel,
        grid=(DEPTH, n_main),
        in_specs=[pl.BlockSpec((1, D_MODEL, D_MODEL), lambda l, c: (l, 0, src_block(c)))],
        out_specs=pl.BlockSpec((1, D_MODEL, D_MODEL), lambda l, c: (l, 0, c)),
        out_shape=jax.ShapeDtypeStruct((DEPTH, D_MODEL, n_main * D_MODEL), BF16),
        compiler_params=_cparams(("parallel", "parallel")),
        name="w_in_relayout",
    )(w_in)


def _scan_tables():
    c = CHUNK
    idx = np.arange(c)
    tri = np.zeros((N_DIR, c, c), np.float32)
    tri[0] = (idx[None, :] <= idx[:, None])
    tri[1] = (idx[None, :] >= idx[:, None])
    pmask = np.zeros((N_DIR, N_LEVELS, c, c), np.float32)
    qrow = np.zeros((N_DIR, N_LEVELS, c, 1), np.float32)
    for lv in range(N_LEVELS):
        m = 1 << lv
        blk = idx // (2 * m)
        upper = (idx // m) % 2 == 1
        same = blk[:, None] == blk[None, :]
        pmask[0, lv] = same & upper[:, None] & ~upper[None, :]
        qrow[0, lv, :, 0] = upper
        pmask[1, lv] = same & ~upper[:, None] & upper[None, :]
        qrow[1, lv, :, 0] = ~upper
    return jnp.asarray(tri), jnp.asarray(pmask), jnp.asarray(qrow)


def _split3(x):
    hi = x.astype(BF16)
    r1 = x - hi.astype(F32)
    mid = r1.astype(BF16)
    lo = (r1 - mid.astype(F32)).astype(BF16)
    return hi, mid, lo


def _dot_table(table3, x):
    return jnp.dot(table3, jnp.concatenate(_split3(x), axis=0), preferred_element_type=F32)


def _level_reference_rows(b, lv, d):
    c, width = b.shape
    m = 1 << lv
    if 2 * m >= 8:
        b3 = b.reshape(c // (2 * m), 2 * m, width)
        ref = jnp.where(d == 0, b3[:, m - 1:m, :], b3[:, m:m + 1, :])
        return jnp.broadcast_to(ref, b3.shape).reshape(c, width)
    assert m == 2
    b3 = b.reshape(c // 8, 8, width)
    first = jnp.where(d == 0, b3[:, 1:2, :], b3[:, 2:3, :])
    second = jnp.where(d == 0, b3[:, 5:6, :], b3[:, 6:7, :])
    r = lax.broadcasted_iota(jnp.int32, (1, 8, 1), 1)
    return jnp.where(r < 4, first, second).reshape(c, width)


def _chunk_index(d, j, n_chunks):
    return j + d * (n_chunks - 1 - 2 * j)


SEQS_PER_STEP = 2


def _hgrn_stages(q_ref, f_ref, v_ref, lb_ref, tri3, pm_ref, qr_ref, o_ref, s_scr, d):
    for s in range(SEQS_PER_STEP):
        x = f_ref[s]
        logsig = jnp.minimum(x, 0.0) - jnp.log(1.0 + jnp.exp(-jnp.abs(x)))
        la = lb_ref[0, 0:1, :]
        lc = lb_ref[0, 1:2, :] + logsig
        logf = jnp.maximum(la, lc) + jnp.log(1.0 + jnp.exp(-jnp.abs(la - lc)))
        kf = jnp.exp(lc - x)
        q = _silu(q_ref[s]) * HEAD_DIM ** -0.5
        v = v_ref[s]
        yield
        b = _dot_table(tri3, logf)
        btot = jnp.sum(logf, axis=0, keepdims=True)
        xs = [jnp.where(qr_ref[0, 0] > 0.5, q * jnp.exp(logf), kf)]
        yield
        for lv in range(1, N_LEVELS):
            e = jnp.exp(-jnp.abs(b - _level_reference_rows(b, lv, d)))
            xs.append(jnp.where(qr_ref[0, lv] > 0.5, q, kf) * e)
            yield
        q_in = q * jnp.exp(b)
        k_out = kf * jnp.exp(btot - b)
        e_tot = jnp.exp(btot)
        yield
        for h in range(N_HEADS):
            hs = slice(h * HEAD_DIM, (h + 1) * HEAD_DIM)
            qh, kh, vh = q[:, hs], kf[:, hs], v[:, hs]
            scores = None
            for lv in range(N_LEVELS):
                xh = xs[lv][:, hs].astype(BF16)
                part = lax.dot_general(xh, xh, (((1,), (1,)), ((), ())), preferred_element_type=F32)
                scores = jnp.where(pm_ref[0, lv] > 0.5, part, 0.0 if scores is None else scores)
            s_prev = s_scr[s, h]
            diag = jnp.sum(qh * kh, axis=-1, keepdims=True)
            o_ref[0, s, :, hs] = (_bdot(scores, vh) + diag * vh + _bdot(q_in[:, hs], s_prev)).astype(BF16)
            e_col = jnp.transpose(jnp.broadcast_to(e_tot[:, hs], (HEAD_DIM, HEAD_DIM)))[:, 0:1]
            s_scr[s, h] = e_col * s_prev + _bdot_tn(k_out[:, hs], vh)
            yield


def _gdn_stages(q_ref, k_ref, v_ref, tail_ref, par_ref, tri3, pm_ref, o_ref, s_scr, d):
    c = CHUNK
    incl = tri3[:, :c].astype(F32) > 0.5
    eye = (lax.broadcasted_iota(jnp.int32, (c, c), 0) == lax.broadcasted_iota(jnp.int32, (c, c), 1))
    strict = jnp.logical_and(incl, jnp.logical_not(eye))

    def pick(arr_fn, base, h):
        return jnp.where(d == 0, arr_fn(base + h), arr_fn(base + N_HEADS + h))

    a_mats, qks, rhss, g_cols, g_tots, where = [], [], [], [], [], []
    for s in range(SEQS_PER_STEP):
        tail = tail_ref[s]
        beta_all = _sigmoid(tail)
        xg = tail + par_ref[1:2, :]
        softplus = jnp.maximum(xg, 0.0) + jnp.log1p(jnp.exp(-jnp.abs(xg)))
        g_all = par_ref[0:1, :] * softplus
        gcum = _dot_table(tri3, g_all)
        gcum_t = jnp.transpose(gcum)
        gtot = jnp.sum(g_all, axis=0, keepdims=True)
        for h in range(N_HEADS):
            hs = slice(h * HEAD_DIM, (h + 1) * HEAD_DIM)
            qh, kh, vh = q_ref[s, :, hs], k_ref[s, :, hs], v_ref[s, :, hs]
            beta = pick(lambda i: beta_all[:, i:i + 1], TAIL_BETA, h)
            g_c = pick(lambda i: gcum[:, i:i + 1], TAIL_DECAY, h)
            g_r = pick(lambda i: gcum_t[i:i + 1, :], TAIL_DECAY, h)
            g_t = pick(lambda i: gtot[:, i:i + 1], TAIL_DECAY, h)
            decay = jnp.where(incl, jnp.exp(jnp.where(incl, g_c - g_r, 0.0)), 0.0)
            qk_kk = _bdot_nt(jnp.concatenate([qh, kh], axis=0), kh) * jnp.concatenate([decay, decay], axis=0)
            qks.append(qk_kk[:c])
            a_mats.append(jnp.where(strict, beta * qk_kk[c:], 0.0))
            rhss.append(jnp.concatenate([vh * beta, kh * (beta * jnp.exp(g_c))], axis=1))
            g_cols.append(g_c)
            g_tots.append(g_t)
            where.append((s, h))
            if h % 2 == 1:
                yield

    n_invs = [-(pm_ref[0, 0] * a) for a in a_mats]
    a_bf = [a.astype(BF16) for a in a_mats]
    for lv in range(1, N_LEVELS):
        pm = pm_ref[0, lv]
        firsts = [pm * (a + jnp.dot(n.astype(BF16), ab, preferred_element_type=F32))
                  for n, a, ab in zip(n_invs, a_mats, a_bf)]
        yield
        n_invs = [n - e - _bdot(e, n) for n, e in zip(n_invs, firsts)]
        yield
    uws = [r + _bdot(n, r) for n, r in zip(n_invs, rhss)]
    yield

    lefts, o_adds, s_adds = [], [], []
    for i, (s, h) in enumerate(where):
        hs = slice(h * HEAD_DIM, (h + 1) * HEAD_DIM)
        qh, kh = q_ref[s, :, hs], k_ref[s, :, hs]
        g_c, g_t = g_cols[i], g_tots[i]
        qk_uw = _bdot(qks[i], uws[i])
        kd_uw = _bdot_tn(kh * jnp.exp(g_t - g_c), uws[i])
        lefts.append(jnp.concatenate([qh * jnp.exp(g_c) - qk_uw[:, HEAD_DIM:], kd_uw[:, HEAD_DIM:]], axis=0))
        o_adds.append(qk_uw[:, :HEAD_DIM])
        s_adds.append(kd_uw[:, :HEAD_DIM])
        if i % 2 == 1:
            yield

    for i, (s, h) in enumerate(where):
        hs = slice(h * HEAD_DIM, (h + 1) * HEAD_DIM)
        s_prev = s_scr[s, h]
        both = _bdot(lefts[i], s_prev)
        o_ref[0, s, :, hs] = (both[:c] + o_adds[i]).astype(BF16)
        s_scr[s, h] = jnp.exp(g_tots[i]) * s_prev - both[c:] + s_adds[i]
        if i % 2 == 1:
            yield


def _mixer_kernel(*refs, has_s0, has_out):
    it = iter(refs)
    zq_ref, zf_ref, zv_ref, lb_ref, tri3_ref, pm_ref, qr_ref = (next(it) for _ in range(7))
    gq_ref, gk_ref, gv_ref, tail_ref, par_ref = (next(it) for _ in range(5))
    s0h_ref, s0g_ref = (next(it), next(it)) if has_s0 else (None, None)
    if has_out:
        next(it), next(it)
    oh_ref, og_ref = next(it), next(it)
    sfh_ref, sfg_ref = (next(it), next(it)) if has_out else (None, None)
    sh_scr, sg_scr = next(it), next(it)
    d = pl.program_id(0)

    @pl.when(pl.program_id(2) == 0)
    def _():
        if has_s0:
            sh_scr[...] = s0h_ref[:, 0, 0]
            sg_scr[...] = s0g_ref[:, 0, 0]
        else:
            sh_scr[...] = jnp.zeros_like(sh_scr)
            sg_scr[...] = jnp.zeros_like(sg_scr)

    tri3 = tri3_ref[0]
    pending = [_gdn_stages(gq_ref, gk_ref, gv_ref, tail_ref, par_ref, tri3, pm_ref, og_ref, sg_scr, d),
               _hgrn_stages(zq_ref, zf_ref, zv_ref, lb_ref, tri3, pm_ref, qr_ref, oh_ref, sh_scr, d)]
    while pending:
        for g in list(pending):
            try:
                next(g)
            except StopIteration:
                pending.remove(g)

    if has_out:
        @pl.when(pl.program_id(2) == pl.num_programs(2) - 1)
        def _():
            sfh_ref[:, 0, 0] = sh_scr[...]
            sfg_ref[:, 0, 0] = sg_scr[...]


def _mixer_call(z, lb_logs, gdn_par, tables, s_h0, s_g0, layer, n_seq, seq, state_acc):
    has_out = state_acc is not None
    ns = SEQS_PER_STEP
    n_chunks = seq // CHUNK
    has_s0 = s_h0 is not None
    tri, pmask, qrow = tables
    tri3 = jnp.concatenate([tri, tri, tri], axis=-1).astype(BF16)
    z3 = z.reshape(n_seq, seq, D_IN_PAD)
    rows = lambda d, b, j: (b, _chunk_index(d, j, n_chunks))
    blk = lambda width, col: pl.BlockSpec((ns, CHUNK, width), lambda d, b, j: (*rows(d, b, j), col(d)))
    state_in = pl.BlockSpec((ns, 1, 1, N_HEADS, HEAD_DIM, HEAD_DIM), lambda d, b, j: (b, layer, d, 0, 0, 0))
    state_out = pl.BlockSpec((ns, 1, 1, N_HEADS, HEAD_DIM, HEAD_DIM), lambda d, b, j: (b, layer, d, 0, 0, 0))
    o_spec = pl.BlockSpec((1, ns, CHUNK, D_MODEL), lambda d, b, j: (d, *rows(d, b, j), 0))
    o_shape = jax.ShapeDtypeStruct((N_DIR, n_seq, seq, D_MODEL), BF16)
    s_shape = jax.ShapeDtypeStruct((n_seq, DEPTH, N_DIR, N_HEADS, HEAD_DIM, HEAD_DIM), F32)
    in_specs = [blk(D_MODEL, lambda d: COL_Q), blk(D_MODEL, lambda d: COL_F + d), blk(D_MODEL, lambda d: COL_I),
                pl.BlockSpec((1, 2, D_MODEL), lambda d, b, j: (d, 0, 0)),
                pl.BlockSpec((1, CHUNK, 3 * CHUNK), lambda d, b, j: (d, 0, 0)),
                pl.BlockSpec((1, N_LEVELS, CHUNK, CHUNK), lambda d, b, j: (d, 0, 0, 0)),
                pl.BlockSpec((1, N_LEVELS, CHUNK, 1), lambda d, b, j: (d, 0, 0, 0)),
                blk(D_MODEL, lambda d: COL_QKV), blk(D_MODEL, lambda d: COL_QKV + 1),
                blk(D_MODEL, lambda d: COL_QKV + 2),
                blk(128, lambda d: COL_TAIL // 128),
                pl.BlockSpec((2, 128), lambda d, b, j: (0, 0))]
    args = [z3, z3, z3, lb_logs, tri3, pmask, qrow, z3, z3, z3, z3, gdn_par]
    if has_s0:
        in_specs += [state_in, state_in]
        args += [s_h0, s_g0]
    aliases = {}
    if has_out:
        aliases = {len(args): 2, len(args) + 1: 3}
        in_specs += [pl.BlockSpec(memory_space=pl.ANY)] * 2
        args += list(state_acc)
    res = pl.pallas_call(
        functools.partial(_mixer_kernel, has_s0=has_s0, has_out=has_out),
        grid=(N_DIR, n_seq // ns, n_chunks),
        in_specs=in_specs,
        out_specs=[o_spec, o_spec] + ([state_out, state_out] if has_out else []),
        out_shape=[o_shape, o_shape] + ([s_shape, s_shape] if has_out else []),
        scratch_shapes=[pltpu.VMEM((ns, N_HEADS, HEAD_DIM, HEAD_DIM), F32)] * 2,
        input_output_aliases=aliases,
        compiler_params=_cparams(("parallel", "parallel", "arbitrary")),
        name="mixer_scan",
    )(*args)
    oh = res[0].reshape(N_DIR, n_seq * seq, D_MODEL)
    og = res[1].reshape(N_DIR, n_seq * seq, D_MODEL)
    return (oh, og, (res[2], res[3])) if has_out else (oh, og, None)


def _head_rmsnorm(x, gain):
    parts = []
    for h in range(N_HEADS):
        xh = x[:, h * HEAD_DIM:(h + 1) * HEAD_DIM]
        parts.append(xh * lax.rsqrt(jnp.mean(xh * xh, axis=-1, keepdims=True) + EPS))
    return jnp.concatenate(parts, axis=-1) * gain


def _post_kernel(oh_ref, og_ref, zoh_ref, zog_ref, zgh_ref, zgg_ref, x_ref, g1_ref, hn_ref, gn_ref,
                 wph_ref, wpg_ref, wo_ref, o_ref):
    oh = _head_rmsnorm(oh_ref[0].astype(F32) + oh_ref[1].astype(F32), hn_ref[...]) * _silu(zoh_ref[...])
    og = _head_rmsnorm(og_ref[0].astype(F32) + og_ref[1].astype(F32), gn_ref[...]) * _silu(zog_ref[...])
    y = (_sigmoid(zgh_ref[...]) * _bdot(oh, wph_ref[...])
         + _sigmoid(zgg_ref[...]) * _bdot(og, wpg_ref[...]))
    o_ref[...] = x_ref[...] + g1_ref[0] * _bdot(y, wo_ref[...])


def _post_call(oh, og, z, x, mod, hn, gn, wph, wpg, wo, rows_per_mod, tm):
    n = x.shape[0]
    mod_idx = lambda i: (i * tm) // rows_per_mod
    zspec = lambda col: pl.BlockSpec((tm, D_MODEL), lambda i, col=col: (i, col))
    wspec = pl.BlockSpec((D_MODEL, D_MODEL), lambda i: (0, 0))
    vspec = pl.BlockSpec((1, D_MODEL), lambda i: (0, 0))
    return pl.pallas_call(
        _post_kernel,
        grid=(n // tm,),
        in_specs=[pl.BlockSpec((N_DIR, tm, D_MODEL), lambda i: (0, i, 0)),
                  pl.BlockSpec((N_DIR, tm, D_MODEL), lambda i: (0, i, 0)),
                  zspec(COL_OGH), zspec(COL_OGG), zspec(COL_GATE_H), zspec(COL_GATE_G),
                  pl.BlockSpec((tm, D_MODEL), lambda i: (i, 0)),
                  pl.BlockSpec((1, 1, D_MODEL), lambda i: (mod_idx(i), 0, 2)),
                  vspec, vspec, wspec, wspec, wspec],
        out_specs=pl.BlockSpec((tm, D_MODEL), lambda i: (i, 0)),
        out_shape=jax.ShapeDtypeStruct((n, D_MODEL), F32),
        compiler_params=_cparams(("parallel",)),
        name="mix_out",
    )(oh, og, z, z, z, z, x, mod, hn, gn, wph, wpg, wo)


FFN_SUB = 256
FFN_TILE = 256


def _ffn_up_kernel(x_ref, sh_ref, sc_ref, gain_ref, wab_ref, cw_ref, cb_ref, o_ref, h_scr, *, rows, cols):
    tr = x_ref.shape[0]
    tf = o_ref.shape[1]
    sub = FFN_SUB
    n_sub = tr // sub

    @pl.when(pl.program_id(1) == 0)
    def _():
        h_scr[...] = _norm_mod(x_ref[...], gain_ref[...], sh_ref[0], sc_ref[0]).astype(BF16)

    t = lax.broadcasted_iota(jnp.int32, (sub, 1), 0)
    col = t & (cols - 1)
    first_col, last_col = col >= 1, col <= cols - 2
    w = [[cw_ref[dr, dc:dc + 1, :] for dc in range(3)] for dr in range(3)]
    bias = cb_ref[...]

    def project(i):
        ab = jnp.dot(h_scr[i * sub:(i + 1) * sub, :], wab_ref[...], preferred_element_type=F32)
        a, gate = ab[:, :tf], ab[:, tf:]
        a_l = jnp.where(first_col, pltpu.roll(a, 1, 0), 0.0)
        a_r = jnp.where(last_col, pltpu.roll(a, sub - 1, 0), 0.0)
        taps = [a_l * w[dr][0] + a * w[dr][1] + a_r * w[dr][2] for dr in ((0, 1, 2) if rows > 1 else (1,))]
        return taps, gate

    def finish(i, prev, cur, nxt):
        taps, gate = cur
        if rows > 1:
            up_src, mid, dn_src = taps
            r = ((t + i * sub) >> int(math.log2(cols))) & (rows - 1)
            zeros = jnp.zeros((cols, tf), F32)
            above = jnp.concatenate([prev[0][0][sub - cols:] if prev is not None else zeros, up_src[:sub - cols]], axis=0)
            below = jnp.concatenate([dn_src[cols:], nxt[0][2][:cols] if nxt is not None else zeros], axis=0)
            if tr > rows * cols:
                above, below = jnp.where(r >= 1, above, 0.0), jnp.where(r <= rows - 2, below, 0.0)
            conv = mid + above + below
        else:
            conv = taps[0]
        o_ref[i * sub:(i + 1) * sub, :] = (_silu(conv + bias) * gate).astype(BF16)

    blocks = [None] * n_sub
    for i in range(n_sub):
        blocks[i] = project(i)
        if i >= 1:
            finish(i - 1, blocks[i - 2] if i >= 2 else None, blocks[i - 1], blocks[i])
    finish(n_sub - 1, blocks[n_sub - 2] if n_sub >= 2 else None, blocks[n_sub - 1], None)


def _ffn_down_kernel(a_ref, x_ref, g2_ref, wo_ref, *rest, final_norm):
    if final_norm:
        nf_ref, o_ref = rest
    else:
        (o_ref,) = rest
    y = x_ref[...] + g2_ref[0] * jnp.dot(a_ref[...], wo_ref[...], preferred_element_type=F32)
    if final_norm:
        y = y * lax.rsqrt(jnp.mean(y * y, axis=-1, keepdims=True) + EPS) * nf_ref[...]
    o_ref[...] = y


def _ffn_call(x, mod, gain, w_ab, cw, cb, w_out, rows_per_mod, tr, rows, cols, norm_final):
    n = x.shape[0]
    tf = FFN_TILE
    nj = D_FF // tf
    mod_idx = lambda i: (i * tr) // rows_per_mod
    mspec = lambda k: pl.BlockSpec((1, 1, D_MODEL), lambda i, j, k=k: (mod_idx(i), 0, k))
    assert tr % FFN_SUB == 0 and FFN_SUB % cols == 0
    act = pl.pallas_call(
        functools.partial(_ffn_up_kernel, rows=rows, cols=cols),
        grid=(n // tr, nj),
        in_specs=[pl.BlockSpec((tr, D_MODEL), lambda i, j: (i, 0)),
                  mspec(3), mspec(4),
                  pl.BlockSpec((1, D_MODEL), lambda i, j: (0, 0)),
                  pl.BlockSpec((D_MODEL, 2 * tf), lambda i, j: (0, j)),
                  pl.BlockSpec((3, 3, tf), lambda i, j: (0, 0, j)),
                  pl.BlockSpec((1, tf), lambda i, j: (0, j))],
        out_specs=pl.BlockSpec((tr, tf), lambda i, j: (i, j)),
        out_shape=jax.ShapeDtypeStruct((n, D_FF), BF16),
        scratch_shapes=[pltpu.VMEM((tr, D_MODEL), BF16)],
        compiler_params=_cparams(("parallel", "arbitrary")),
        name="ffn_up",
    )(x, mod, mod, gain, w_ab, cw, cb)

    tm = min(rows_per_mod, 512)
    mod_idx2 = lambda i: (i * tm) // rows_per_mod
    final_norm = norm_final is not None
    in_specs = [pl.BlockSpec((tm, D_FF), lambda i: (i, 0)),
                pl.BlockSpec((tm, D_MODEL), lambda i: (i, 0)),
                pl.BlockSpec((1, 1, D_MODEL), lambda i: (mod_idx2(i), 0, 5)),
                pl.BlockSpec((D_FF, D_MODEL), lambda i: (0, 0))]
    args = [act, x, mod, w_out]
    if final_norm:
        in_specs.append(pl.BlockSpec((1, D_MODEL), lambda i: (0, 0)))
        args.append(norm_final)
    return pl.pallas_call(
        functools.partial(_ffn_down_kernel, final_norm=final_norm),
        grid=(n // tm,),
        in_specs=in_specs,
        out_specs=pl.BlockSpec((tm, D_MODEL), lambda i: (i, 0)),
        out_shape=jax.ShapeDtypeStruct((n, D_MODEL), F32),
        compiler_params=_cparams(("parallel",)),
        name="ffn_down",
    )(*args)


def _run_pass(x, mod_of_layer, rows_per_mod, n_seq, seq, conv_rows, conv_cols, s_h0, s_g0, want_states,
              prm, tables):
    n = n_seq * seq
    assert conv_rows & (conv_rows - 1) == 0 and conv_cols & (conv_cols - 1) == 0
    ffn_rows = min(rows_per_mod, max(seq, 2048))
    assert ffn_rows % seq == 0 and n % ffn_rows == 0
    state_shape = (n_seq, DEPTH, N_DIR, N_HEADS, HEAD_DIM, HEAD_DIM)
    states = (jnp.zeros(state_shape, F32), jnp.zeros(state_shape, F32)) if want_states else None
    for l in range(DEPTH):
        mod = mod_of_layer(l)
        z = _inproj_call(x, mod, prm["norm_mix"][l], prm["w_in"][l], prm["w_in_tail"][l], prm["gdn_conv"][l],
                         rows_per_mod, seq)
        oh, og, states = _mixer_call(z, prm["lb_logs"][l], prm["gdn_par"][l], tables, s_h0, s_g0, l,
                                     n_seq, seq, states)
        x = _post_call(oh, og, z, x, mod, prm["hgrn_norm"][l], prm["gdn_norm"][l], prm["w_proj_h"][l],
                       prm["w_proj_g"][l], prm["w_out"][l], rows_per_mod, tm=256)
        x = _ffn_call(x, mod, prm["norm_ffn"][l], prm["w_ffn_in"][l], prm["ffn_conv"][l], prm["ffn_conv_b"][l],
                      prm["w_ffn_out"][l], rows_per_mod, ffn_rows, conv_rows, conv_cols,
                      prm["norm_final"] if l == DEPTH - 1 else None)
    return x, states


def kernel(x_prompt, x_sample, c, state_hgrn, state_gdn, c_ctx, w_ada, b_ada, norm_mix, norm_ffn, w_in, hgrn_lb, hgrn_norm, gdn_conv, gdn_a_log, gdn_dt_bias, gdn_norm, w_proj_h, w_proj_g, w_out, w_ffn_in, ffn_conv, ffn_conv_b, w_ffn_out, norm_final):
    ctx_b, ctx_len, _ = x_prompt.shape
    dec_b, dec_len, _ = x_sample.shape

    lb_cum = jnp.cumsum(jax.nn.softmax(hgrn_lb.astype(F32), axis=0), axis=0)
    lower = lb_cum - lb_cum[0]
    lb_logs = jnp.stack([jnp.log(lower), jnp.log1p(-lower)], axis=2)
    small = 2 * N_DIR * N_HEADS
    tail_at = COL_GATE_H * D_MODEL
    w_in_tail = jnp.concatenate(
        [w_in[..., tail_at + small:], w_in[..., tail_at:tail_at + small],
         jnp.zeros((DEPTH, D_MODEL, TAIL_W - small), w_in.dtype)], axis=-1).astype(BF16)
    nj = D_FF // FFN_TILE
    w_ffn_ab = jnp.concatenate([w_ffn_in[..., :D_FF].reshape(DEPTH, D_MODEL, nj, FFN_TILE),
                                w_ffn_in[..., D_FF:].reshape(DEPTH, D_MODEL, nj, FFN_TILE)],
                               axis=-1).reshape(DEPTH, D_MODEL, 2 * D_FF).astype(BF16)
    gdn_par = jnp.zeros((DEPTH, 2, 128), F32)
    gdn_par = gdn_par.at[:, 0, TAIL_DECAY:TAIL_DECAY + N_DIR * N_HEADS].set(
        -jnp.exp(gdn_a_log.astype(F32)).reshape(DEPTH, -1))
    gdn_par = gdn_par.at[:, 1, TAIL_DECAY:TAIL_DECAY + N_DIR * N_HEADS].set(
        gdn_dt_bias.astype(F32).reshape(DEPTH, -1))
    prm = dict(
        norm_mix=norm_mix.reshape(DEPTH, 1, D_MODEL), norm_ffn=norm_ffn.reshape(DEPTH, 1, D_MODEL),
        w_in=_w_in_main(w_in), w_in_tail=w_in_tail, lb_logs=lb_logs, gdn_conv=gdn_conv, gdn_par=gdn_par,
        hgrn_norm=jnp.tile(hgrn_norm, (1, N_HEADS)).reshape(DEPTH, 1, D_MODEL),
        gdn_norm=jnp.tile(gdn_norm, (1, N_HEADS)).reshape(DEPTH, 1, D_MODEL),
        w_proj_h=w_proj_h.astype(BF16), w_proj_g=w_proj_g.astype(BF16), w_out=w_out.astype(BF16),
        w_ffn_in=w_ffn_ab, ffn_conv=ffn_conv, ffn_conv_b=ffn_conv_b.reshape(DEPTH, 1, D_FF),
        w_ffn_out=w_ffn_out.astype(BF16), norm_final=norm_final.reshape(1, D_MODEL))
    tables = _scan_tables()

    cond = jnp.concatenate([c_ctx[None], c, jnp.zeros((16 - 1 - dec_b, D_MODEL), F32)], axis=0)
    mod = _ada_call(cond, w_ada, b_ada)

    n_ctx = ctx_b * ctx_len
    yc, (new_h, new_g) = _run_pass(x_prompt.reshape(n_ctx, D_MODEL), lambda l: mod[l, 0:1, None, :], n_ctx,
                           ctx_b, ctx_len, 1, ctx_len, None, None, True, prm, tables)
    n_dec = dec_b * dec_len
    yd, _ = _run_pass(x_sample.reshape(n_dec, D_MODEL), lambda l: mod[l, 1:1 + dec_b, None, :], dec_len,
                         dec_b, dec_len, dec_len // GRID_W, GRID_W, state_hgrn, state_gdn, False, prm, tables)
    return (yc.reshape(ctx_b, ctx_len, D_MODEL), yd.reshape(dec_b, dec_len, D_MODEL),
            new_h, new_g)
```

```python
import functools
import math

import jax
import jax.numpy as jnp
import numpy as np
from jax import lax
from jax.experimental import pallas as pl
from jax.experimental.pallas import tpu as pltpu

F32 = jnp.float32
BF16 = jnp.bfloat16

D_MODEL = 1024
DEPTH = 4
GRID_W = 64
N_HEADS = 8
HEAD_DIM = 128
D_FF = 2816
QKV_CONV = 5
N_DIR = 2
N_MOD = 6
EPS = 1e-6

CHUNK = 128
N_LEVELS = int(math.log2(CHUNK))

COL_QKV, COL_Q, COL_F, COL_I, COL_OGH, COL_OGG, COL_GATE_H, COL_GATE_G = 0, 3, 4, 6, 7, 8, 9, 10
COL_TAIL = 11 * D_MODEL
TAIL_W = 256
D_IN_PAD = COL_TAIL + TAIL_W
TAIL_BETA, TAIL_DECAY = 0, N_DIR * N_HEADS

VMEM_LIMIT_BYTES = 60 * 1024 * 1024


def _cparams(sem):
    return pltpu.CompilerParams(dimension_semantics=sem, vmem_limit_bytes=VMEM_LIMIT_BYTES)


def _bdot(a, b):
    return jnp.dot(a.astype(BF16), b.astype(BF16), preferred_element_type=F32)


def _bdot_nt(a, b):
    return lax.dot_general(a.astype(BF16), b.astype(BF16), (((1,), (1,)), ((), ())),
                           preferred_element_type=F32)


def _bdot_tn(a, b):
    return lax.dot_general(a.astype(BF16), b.astype(BF16), (((0,), (0,)), ((), ())),
                           preferred_element_type=F32)


def _sigmoid(x):
    return 1.0 / (1.0 + jnp.exp(-x))


def _silu(x):
    return x * _sigmoid(x)


def _ada_kernel(c_ref, w_ref, b_ref, o_ref):
    o_ref[0] = _bdot(_silu(c_ref[...]), w_ref[0]) + b_ref[0]


def _ada_call(cond, w_ada, b_ada):
    rows = cond.shape[0]
    tn = 1536
    return pl.pallas_call(
        _ada_kernel,
        grid=(DEPTH, N_MOD * D_MODEL // tn),
        in_specs=[pl.BlockSpec((rows, D_MODEL), lambda l, j: (0, 0)),
                  pl.BlockSpec((1, D_MODEL, tn), lambda l, j: (l, 0, j)),
                  pl.BlockSpec((1, 1, tn), lambda l, j: (l, 0, j))],
        out_specs=pl.BlockSpec((1, rows, tn), lambda l, j: (l, 0, j)),
        out_shape=jax.ShapeDtypeStruct((DEPTH, rows, N_MOD * D_MODEL), F32),
        compiler_params=_cparams(("parallel", "parallel")),
        name="ada_mod",
    )(cond, w_ada, b_ada.reshape(DEPTH, 1, N_MOD * D_MODEL))


def _norm_mod(x, gain, shift, scale):
    y = x * lax.rsqrt(jnp.mean(x * x, axis=-1, keepdims=True) + EPS)
    return (y * gain) * (1.0 + scale) + shift


INPROJ_TN = 768


def _inproj_kernel(x_ref, sh_ref, sc_ref, g_ref, w_ref, wt_ref, cw_ref, o_ref, h_scr, *, seq):
    j = pl.program_id(1)
    tm = x_ref.shape[0]
    conv_tiles = 3 * D_MODEL // INPROJ_TN
    main_tiles = COL_GATE_H * D_MODEL // INPROJ_TN

    @pl.when(j == 0)
    def _():
        h_scr[...] = _norm_mod(x_ref[...], g_ref[...], sh_ref[0], sc_ref[0]).astype(BF16)

    @pl.when(jnp.logical_and(j >= conv_tiles, j < main_tiles))
    def _():
        o_ref[...] = jnp.dot(h_scr[...], w_ref[...], preferred_element_type=F32)

    @pl.when(j >= main_tiles)
    def _():
        o_ref[...] = jnp.dot(h_scr[...], wt_ref[...], preferred_element_type=F32)

    @pl.when(j < conv_tiles)
    def _():
        z = jnp.dot(h_scr[...], w_ref[...], preferred_element_type=F32)
        t = lax.broadcasted_iota(jnp.int32, (tm, 1), 0) & (seq - 1)
        pad = QKV_CONV // 2
        for i in range(INPROJ_TN // HEAD_DIM):
            cs = slice(i * HEAD_DIM, (i + 1) * HEAD_DIM)
            zi = z[:, cs]
            acc = zi * cw_ref[pad:pad + 1, cs]
            for s in range(1, pad + 1):
                prev = jnp.where(t >= s, pltpu.roll(zi, s, 0), 0.0)
                nxt = jnp.where(t < seq - s, pltpu.roll(zi, tm - s, 0), 0.0)
                acc = acc + prev * cw_ref[pad - s:pad - s + 1, cs] + nxt * cw_ref[pad + s:pad + s + 1, cs]
            y = _silu(acc)
            head = j * (INPROJ_TN // HEAD_DIM) + i
            scale = jnp.where(head < N_HEADS, HEAD_DIM ** -0.5, 1.0)
            normed = y * (lax.rsqrt(jnp.sum(y * y, axis=-1, keepdims=True) + EPS) * scale)
            o_ref[:, cs] = jnp.where(head >= 2 * N_HEADS, y, normed)


def _inproj_call(x, mod, gain, w, w_tail, conv_w, rows_per_mod, seq):
    n = x.shape[0]
    tm = min(rows_per_mod, 2048)
    tn = INPROJ_TN
    assert tm % seq == 0 and n % tm == 0 and seq & (seq - 1) == 0
    conv_tiles = 3 * D_MODEL // tn
    main_tiles = w.shape[1] // tn
    tail_tiles = w_tail.shape[1] // tn
    assert (main_tiles + tail_tiles) * tn == D_IN_PAD and main_tiles * tn == COL_GATE_H * D_MODEL
    mod_idx = lambda i: (i * tm) // rows_per_mod
    return pl.pallas_call(
        functools.partial(_inproj_kernel, seq=seq),
        grid=(n // tm, D_IN_PAD // tn),
        in_specs=[pl.BlockSpec((tm, D_MODEL), lambda i, j: (i, 0)),
                  pl.BlockSpec((1, 1, D_MODEL), lambda i, j: (mod_idx(i), 0, 0)),
                  pl.BlockSpec((1, 1, D_MODEL), lambda i, j: (mod_idx(i), 0, 1)),
                  pl.BlockSpec((1, D_MODEL), lambda i, j: (0, 0)),
                  pl.BlockSpec((D_MODEL, tn), lambda i, j: (0, jnp.minimum(j, main_tiles - 1))),
                  pl.BlockSpec((D_MODEL, tn), lambda i, j: (0, jnp.clip(j - main_tiles, 0, tail_tiles - 1))),
                  pl.BlockSpec((QKV_CONV, tn), lambda i, j: (0, jnp.minimum(j, conv_tiles - 1)))],
        out_specs=pl.BlockSpec((tm, tn), lambda i, j: (i, j)),
        out_shape=jax.ShapeDtypeStruct((n, D_IN_PAD), F32),
        scratch_shapes=[pltpu.VMEM((tm, D_MODEL), BF16)],
        compiler_params=_cparams(("parallel", "arbitrary")),
        name="in_proj",
    )(x, mod, mod, gain, w, w_tail, conv_w)


def _relayout_kernel(src_ref, o_ref):
    o_ref[...] = src_ref[...].astype(BF16)


def _w_in_main(w_in):
    n_hgrn = COL_OGG - COL_Q
    n_main = COL_GATE_H
    src_block = lambda c: jnp.where(c < COL_Q, c + n_hgrn, jnp.where(c < COL_OGG, c - COL_Q, c))
    return pl.pallas_call(
        _relayout_kernel,
        grid=(DEPTH, n_main),
        in_specs=[pl.BlockSpec((1, D_MODEL, D_MODEL), lambda l, c: (l, 0, src_block(c)))],
        out_specs=pl.BlockSpec((1, D_MODEL, D_MODEL), lambda l, c: (l, 0, c)),
        out_shape=jax.ShapeDtypeStruct((DEPTH, D_MODEL, n_main * D_MODEL), BF16),
        compiler_params=_cparams(("parallel", "parallel")),
        name="w_in_relayout",
    )(w_in)


def _scan_tables():
    c = CHUNK
    idx = np.arange(c)
    tri = np.zeros((N_DIR, c, c), np.float32)
    tri[0] = (idx[None, :] <= idx[:, None])
    tri[1] = (idx[None, :] >= idx[:, None])
    pmask = np.zeros((N_DIR, N_LEVELS, c, c), np.float32)
    qrow = np.zeros((N_DIR, N_LEVELS, c, 1), np.float32)
    for lv in range(N_LEVELS):
        m = 1 << lv
        blk = idx // (2 * m)
        upper = (idx // m) % 2 == 1
        same = blk[:, None] == blk[None, :]
        pmask[0, lv] = same & upper[:, None] & ~upper[None, :]
        qrow[0, lv, :, 0] = upper
        pmask[1, lv] = same & ~upper[:, None] & upper[None, :]
        qrow[1, lv, :, 0] = ~upper
    return jnp.asarray(tri), jnp.asarray(pmask), jnp.asarray(qrow)


def _split3(x):
    hi = x.astype(BF16)
    r1 = x - hi.astype(F32)
    mid = r1.astype(BF16)
    lo = (r1 - mid.astype(F32)).astype(BF16)
    return hi, mid, lo


def _dot_table(table3, x):
    return jnp.dot(table3, jnp.concatenate(_split3(x), axis=0), preferred_element_type=F32)


def _level_reference_rows(b, lv, d):
    c, width = b.shape
    m = 1 << lv
    if 2 * m >= 8:
        b3 = b.reshape(c // (2 * m), 2 * m, width)
        ref = jnp.where(d == 0, b3[:, m - 1:m, :], b3[:, m:m + 1, :])
        return jnp.broadcast_to(ref, b3.shape).reshape(c, width)
    assert m == 2
    b3 = b.reshape(c // 8, 8, width)
    first = jnp.where(d == 0, b3[:, 1:2, :], b3[:, 2:3, :])
    second = jnp.where(d == 0, b3[:, 5:6, :], b3[:, 6:7, :])
    r = lax.broadcasted_iota(jnp.int32, (1, 8, 1), 1)
    return jnp.where(r < 4, first, second).reshape(c, width)


def _chunk_index(d, j, n_chunks):
    return j + d * (n_chunks - 1 - 2 * j)


SEQS_PER_STEP = 2


def _hgrn_stages(q_ref, f_ref, v_ref, lb_ref, tri3, pm_ref, qr_ref, o_ref, s_scr, d):
    for s in range(SEQS_PER_STEP):
        x = f_ref[s]
        logsig = jnp.minimum(x, 0.0) - jnp.log(1.0 + jnp.exp(-jnp.abs(x)))
        la = lb_ref[0, 0:1, :]
        lc = lb_ref[0, 1:2, :] + logsig
        logf = jnp.maximum(la, lc) + jnp.log(1.0 + jnp.exp(-jnp.abs(la - lc)))
        kf = jnp.exp(lc - x)
        q = _silu(q_ref[s]) * HEAD_DIM ** -0.5
        v = v_ref[s]
        yield
        b = _dot_table(tri3, logf)
        btot = jnp.sum(logf, axis=0, keepdims=True)
        xs = [jnp.where(qr_ref[0, 0] > 0.5, q * jnp.exp(logf), kf)]
        yield
        for lv in range(1, N_LEVELS):
            e = jnp.exp(-jnp.abs(b - _level_reference_rows(b, lv, d)))
            xs.append(jnp.where(qr_ref[0, lv] > 0.5, q, kf) * e)
            yield
        q_in = q * jnp.exp(b)
        k_out = kf * jnp.exp(btot - b)
        e_tot = jnp.exp(btot)
        yield
        for h in range(N_HEADS):
            hs = slice(h * HEAD_DIM, (h + 1) * HEAD_DIM)
            qh, kh, vh = q[:, hs], kf[:, hs], v[:, hs]
            scores = None
            for lv in range(N_LEVELS):
                xh = xs[lv][:, hs].astype(BF16)
                part = lax.dot_general(xh, xh, (((1,), (1,)), ((), ())), preferred_element_type=F32)
                scores = jnp.where(pm_ref[0, lv] > 0.5, part, 0.0 if scores is None else scores)
            yield
            s_prev = s_scr[s, h]
            diag = jnp.sum(qh * kh, axis=-1, keepdims=True)
            o_ref[0, s, :, hs] = (_bdot(scores, vh) + diag * vh + _bdot(q_in[:, hs], s_prev)).astype(BF16)
            e_col = jnp.transpose(jnp.broadcast_to(e_tot[:, hs], (HEAD_DIM, HEAD_DIM)))[:, 0:1]
            s_scr[s, h] = e_col * s_prev + _bdot_tn(k_out[:, hs], vh)
            yield


def _gdn_stages(q_ref, k_ref, v_ref, tail_ref, par_ref, tri3, pm_ref, o_ref, s_scr, d):
    c = CHUNK
    incl = tri3[:, :c].astype(F32) > 0.5
    eye = (lax.broadcasted_iota(jnp.int32, (c, c), 0) == lax.broadcasted_iota(jnp.int32, (c, c), 1))
    strict = jnp.logical_and(incl, jnp.logical_not(eye))

    def pick(arr_fn, base, h):
        return jnp.where(d == 0, arr_fn(base + h), arr_fn(base + N_HEADS + h))

    a_mats, qks, rhss, g_cols, g_tots, where = [], [], [], [], [], []
    for s in range(SEQS_PER_STEP):
        tail = tail_ref[s]
        beta_all = _sigmoid(tail)
        xg = tail + par_ref[1:2, :]
        softplus = jnp.maximum(xg, 0.0) + jnp.log1p(jnp.exp(-jnp.abs(xg)))
        g_all = par_ref[0:1, :] * softplus
        gcum = _dot_table(tri3, g_all)
        gcum_t = jnp.transpose(gcum)
        gtot = jnp.sum(g_all, axis=0, keepdims=True)
        for h in range(N_HEADS):
            hs = slice(h * HEAD_DIM, (h + 1) * HEAD_DIM)
            qh, kh, vh = q_ref[s, :, hs], k_ref[s, :, hs], v_ref[s, :, hs]
            beta = pick(lambda i: beta_all[:, i:i + 1], TAIL_BETA, h)
            g_c = pick(lambda i: gcum[:, i:i + 1], TAIL_DECAY, h)
            g_r = pick(lambda i: gcum_t[i:i + 1, :], TAIL_DECAY, h)
            g_t = pick(lambda i: gtot[:, i:i + 1], TAIL_DECAY, h)
            decay = jnp.where(incl, jnp.exp(jnp.where(incl, g_c - g_r, 0.0)), 0.0)
            qk_kk = _bdot_nt(jnp.concatenate([qh, kh], axis=0), kh) * jnp.concatenate([decay, decay], axis=0)
            qks.append(qk_kk[:c])
            a_mats.append(jnp.where(strict, beta * qk_kk[c:], 0.0))
            rhss.append(jnp.concatenate([vh * beta, kh * (beta * jnp.exp(g_c))], axis=1))
            g_cols.append(g_c)
            g_tots.append(g_t)
            where.append((s, h))
            yield

    n_invs = [-(pm_ref[0, 0] * a) for a in a_mats]
    a_bf = [a.astype(BF16) for a in a_mats]
    for lv in range(1, N_LEVELS):
        pm = pm_ref[0, lv]
        firsts = []
        for k, (n, a, ab) in enumerate(zip(n_invs, a_mats, a_bf)):
            firsts.append(pm * (a + jnp.dot(n.astype(BF16), ab, preferred_element_type=F32)))
            if k % 2 == 1:
                yield
        nexts = []
        for k, (n, e) in enumerate(zip(n_invs, firsts)):
            nexts.append(n - e - _bdot(e, n))
            if k % 2 == 1:
                yield
        n_invs = nexts
    uws = [r + _bdot(n, r) for n, r in zip(n_invs, rhss)]
    yield

    lefts, o_adds, s_adds = [], [], []
    for i, (s, h) in enumerate(where):
        hs = slice(h * HEAD_DIM, (h + 1) * HEAD_DIM)
        qh, kh = q_ref[s, :, hs], k_ref[s, :, hs]
        g_c, g_t = g_cols[i], g_tots[i]
        qk_uw = _bdot(qks[i], uws[i])
        kd_uw = _bdot_tn(kh * jnp.exp(g_t - g_c), uws[i])
        lefts.append(jnp.concatenate([qh * jnp.exp(g_c) - qk_uw[:, HEAD_DIM:], kd_uw[:, HEAD_DIM:]], axis=0))
        o_adds.append(qk_uw[:, :HEAD_DIM])
        s_adds.append(kd_uw[:, :HEAD_DIM])
        yield

    for i, (s, h) in enumerate(where):
        hs = slice(h * HEAD_DIM, (h + 1) * HEAD_DIM)
        s_prev = s_scr[s, h]
        both = _bdot(lefts[i], s_prev)
        o_ref[0, s, :, hs] = (both[:c] + o_adds[i]).astype(BF16)
        s_scr[s, h] = jnp.exp(g_tots[i]) * s_prev - both[c:] + s_adds[i]
        yield


def _mixer_kernel(*refs, has_s0, has_out):
    it = iter(refs)
    zq_ref, zf_ref, zv_ref, lb_ref, tri3_ref, pm_ref, qr_ref = (next(it) for _ in range(7))
    gq_ref, gk_ref, gv_ref, tail_ref, par_ref = (next(it) for _ in range(5))
    s0h_ref, s0g_ref = (next(it), next(it)) if has_s0 else (None, None)
    if has_out:
        next(it), next(it)
    oh_ref, og_ref = next(it), next(it)
    sfh_ref, sfg_ref = (next(it), next(it)) if has_out else (None, None)
    sh_scr, sg_scr = next(it), next(it)
    d = pl.program_id(0)

    @pl.when(pl.program_id(2) == 0)
    def _():
        if has_s0:
            sh_scr[...] = s0h_ref[:, 0, 0]
            sg_scr[...] = s0g_ref[:, 0, 0]
        else:
            sh_scr[...] = jnp.zeros_like(sh_scr)
            sg_scr[...] = jnp.zeros_like(sg_scr)

    tri3 = tri3_ref[0]
    pending = [_gdn_stages(gq_ref, gk_ref, gv_ref, tail_ref, par_ref, tri3, pm_ref, og_ref, sg_scr, d),
               _hgrn_stages(zq_ref, zf_ref, zv_ref, lb_ref, tri3, pm_ref, qr_ref, oh_ref, sh_scr, d)]
    while pending:
        for g in list(pending):
            try:
                next(g)
            except StopIteration:
                pending.remove(g)

    if has_out:
        @pl.when(pl.program_id(2) == pl.num_programs(2) - 1)
        def _():
            sfh_ref[:, 0, 0] = sh_scr[...]
            sfg_ref[:, 0, 0] = sg_scr[...]


def _mixer_call(z, lb_logs, gdn_par, tables, s_h0, s_g0, layer, n_seq, seq, state_acc):
    has_out = state_acc is not None
    ns = SEQS_PER_STEP
    n_chunks = seq // CHUNK
    has_s0 = s_h0 is not None
    tri, pmask, qrow = tables
    tri3 = jnp.concatenate([tri, tri, tri], axis=-1).astype(BF16)
    z3 = z.reshape(n_seq, seq, D_IN_PAD)
    rows = lambda d, b, j: (b, _chunk_index(d, j, n_chunks))
    blk = lambda width, col: pl.BlockSpec((ns, CHUNK, width), lambda d, b, j: (*rows(d, b, j), col(d)))
    state_in = pl.BlockSpec((ns, 1, 1, N_HEADS, HEAD_DIM, HEAD_DIM), lambda d, b, j: (b, layer, d, 0, 0, 0))
    state_out = pl.BlockSpec((ns, 1, 1, N_HEADS, HEAD_DIM, HEAD_DIM), lambda d, b, j: (b, layer, d, 0, 0, 0))
    o_spec = pl.BlockSpec((1, ns, CHUNK, D_MODEL), lambda d, b, j: (d, *rows(d, b, j), 0))
    o_shape = jax.ShapeDtypeStruct((N_DIR, n_seq, seq, D_MODEL), BF16)
    s_shape = jax.ShapeDtypeStruct((n_seq, DEPTH, N_DIR, N_HEADS, HEAD_DIM, HEAD_DIM), F32)
    in_specs = [blk(D_MODEL, lambda d: COL_Q), blk(D_MODEL, lambda d: COL_F + d), blk(D_MODEL, lambda d: COL_I),
                pl.BlockSpec((1, 2, D_MODEL), lambda d, b, j: (d, 0, 0)),
                pl.BlockSpec((1, CHUNK, 3 * CHUNK), lambda d, b, j: (d, 0, 0)),
                pl.BlockSpec((1, N_LEVELS, CHUNK, CHUNK), lambda d, b, j: (d, 0, 0, 0)),
                pl.BlockSpec((1, N_LEVELS, CHUNK, 1), lambda d, b, j: (d, 0, 0, 0)),
                blk(D_MODEL, lambda d: COL_QKV), blk(D_MODEL, lambda d: COL_QKV + 1),
                blk(D_MODEL, lambda d: COL_QKV + 2),
                blk(128, lambda d: COL_TAIL // 128),
                pl.BlockSpec((2, 128), lambda d, b, j: (0, 0))]
    args = [z3, z3, z3, lb_logs, tri3, pmask, qrow, z3, z3, z3, z3, gdn_par]
    if has_s0:
        in_specs += [state_in, state_in]
        args += [s_h0, s_g0]
    aliases = {}
    if has_out:
        aliases = {len(args): 2, len(args) + 1: 3}
        in_specs += [pl.BlockSpec(memory_space=pl.ANY)] * 2
        args += list(state_acc)
    res = pl.pallas_call(
        functools.partial(_mixer_kernel, has_s0=has_s0, has_out=has_out),
        grid=(N_DIR, n_seq // ns, n_chunks),
        in_specs=in_specs,
        out_specs=[o_spec, o_spec] + ([state_out, state_out] if has_out else []),
        out_shape=[o_shape, o_shape] + ([s_shape, s_shape] if has_out else []),
        scratch_shapes=[pltpu.VMEM((ns, N_HEADS, HEAD_DIM, HEAD_DIM), F32)] * 2,
        input_output_aliases=aliases,
        compiler_params=_cparams(("parallel", "parallel", "arbitrary")),
        name="mixer_scan",
    )(*args)
    oh = res[0].reshape(N_DIR, n_seq * seq, D_MODEL)
    og = res[1].reshape(N_DIR, n_seq * seq, D_MODEL)
    return (oh, og, (res[2], res[3])) if has_out else (oh, og, None)


def _head_rmsnorm(x, gain):
    parts = []
    for h in range(N_HEADS):
        xh = x[:, h * HEAD_DIM:(h + 1) * HEAD_DIM]
        parts.append(xh * lax.rsqrt(jnp.mean(xh * xh, axis=-1, keepdims=True) + EPS))
    return jnp.concatenate(parts, axis=-1) * gain


def _post_kernel(oh_ref, og_ref, zoh_ref, zog_ref, zgh_ref, zgg_ref, x_ref, g1_ref, hn_ref, gn_ref,
                 wph_ref, wpg_ref, wo_ref, o_ref):
    oh = _head_rmsnorm(oh_ref[0].astype(F32) + oh_ref[1].astype(F32), hn_ref[...]) * _silu(zoh_ref[...])
    og = _head_rmsnorm(og_ref[0].astype(F32) + og_ref[1].astype(F32), gn_ref[...]) * _silu(zog_ref[...])
    y = (_sigmoid(zgh_ref[...]) * _bdot(oh, wph_ref[...])
         + _sigmoid(zgg_ref[...]) * _bdot(og, wpg_ref[...]))
    o_ref[...] = x_ref[...] + g1_ref[0] * _bdot(y, wo_ref[...])


def _post_call(oh, og, z, x, mod, hn, gn, wph, wpg, wo, rows_per_mod, tm):
    n = x.shape[0]
    mod_idx = lambda i: (i * tm) // rows_per_mod
    zspec = lambda col: pl.BlockSpec((tm, D_MODEL), lambda i, col=col: (i, col))
    wspec = pl.BlockSpec((D_MODEL, D_MODEL), lambda i: (0, 0))
    vspec = pl.BlockSpec((1, D_MODEL), lambda i: (0, 0))
    return pl.pallas_call(
        _post_kernel,
        grid=(n // tm,),
        in_specs=[pl.BlockSpec((N_DIR, tm, D_MODEL), lambda i: (0, i, 0)),
                  pl.BlockSpec((N_DIR, tm, D_MODEL), lambda i: (0, i, 0)),
                  zspec(COL_OGH), zspec(COL_OGG), zspec(COL_GATE_H), zspec(COL_GATE_G),
                  pl.BlockSpec((tm, D_MODEL), lambda i: (i, 0)),
                  pl.BlockSpec((1, 1, D_MODEL), lambda i: (mod_idx(i), 0, 2)),
                  vspec, vspec, wspec, wspec, wspec],
        out_specs=pl.BlockSpec((tm, D_MODEL), lambda i: (i, 0)),
        out_shape=jax.ShapeDtypeStruct((n, D_MODEL), F32),
        compiler_params=_cparams(("parallel",)),
        name="mix_out",
    )(oh, og, z, z, z, z, x, mod, hn, gn, wph, wpg, wo)


FFN_SUB = 256
FFN_TILE = 256


def _ffn_up_kernel(x_ref, sh_ref, sc_ref, gain_ref, wab_ref, cw_ref, cb_ref, o_ref, h_scr, *, rows, cols):
    tr = x_ref.shape[0]
    tf = o_ref.shape[1]
    sub = FFN_SUB
    n_sub = tr // sub

    @pl.when(pl.program_id(1) == 0)
    def _():
        h_scr[...] = _norm_mod(x_ref[...], gain_ref[...], sh_ref[0], sc_ref[0]).astype(BF16)

    t = lax.broadcasted_iota(jnp.int32, (sub, 1), 0)
    col = t & (cols - 1)
    first_col, last_col = col >= 1, col <= cols - 2
    w = [[cw_ref[dr, dc:dc + 1, :] for dc in range(3)] for dr in range(3)]
    bias = cb_ref[...]

    def project(i):
        ab = jnp.dot(h_scr[i * sub:(i + 1) * sub, :], wab_ref[...], preferred_element_type=F32)
        a, gate = ab[:, :tf], ab[:, tf:]
        a_l = jnp.where(first_col, pltpu.roll(a, 1, 0), 0.0)
        a_r = jnp.where(last_col, pltpu.roll(a, sub - 1, 0), 0.0)
        taps = [a_l * w[dr][0] + a * w[dr][1] + a_r * w[dr][2] for dr in ((0, 1, 2) if rows > 1 else (1,))]
        return taps, gate

    def finish(i, prev, cur, nxt):
        taps, gate = cur
        if rows > 1:
            up_src, mid, dn_src = taps
            r = ((t + i * sub) >> int(math.log2(cols))) & (rows - 1)
            zeros = jnp.zeros((cols, tf), F32)
            above = jnp.concatenate([prev[0][0][sub - cols:] if prev is not None else zeros, up_src[:sub - cols]], axis=0)
            below = jnp.concatenate([dn_src[cols:], nxt[0][2][:cols] if nxt is not None else zeros], axis=0)
            if tr > rows * cols:
                above, below = jnp.where(r >= 1, above, 0.0), jnp.where(r <= rows - 2, below, 0.0)
            conv = mid + above + below
        else:
            conv = taps[0]
        o_ref[i * sub:(i + 1) * sub, :] = (_silu(conv + bias) * gate).astype(BF16)

    blocks = [None] * n_sub
    for i in range(n_sub):
        blocks[i] = project(i)
        if i >= 1:
            finish(i - 1, blocks[i - 2] if i >= 2 else None, blocks[i - 1], blocks[i])
    finish(n_sub - 1, blocks[n_sub - 2] if n_sub >= 2 else None, blocks[n_sub - 1], None)


def _ffn_down_kernel(a_ref, x_ref, g2_ref, wo_ref, *rest, final_norm):
    if final_norm:
        nf_ref, o_ref = rest
    else:
        (o_ref,) = rest
    y = x_ref[...] + g2_ref[0] * jnp.dot(a_ref[...], wo_ref[...], preferred_element_type=F32)
    if final_norm:
        y = y * lax.rsqrt(jnp.mean(y * y, axis=-1, keepdims=True) + EPS) * nf_ref[...]
    o_ref[...] = y


def _ffn_call(x, mod, gain, w_ab, cw, cb, w_out, rows_per_mod, tr, rows, cols, norm_final):
    n = x.shape[0]
    tf = FFN_TILE
    nj = D_FF // tf
    mod_idx = lambda i: (i * tr) // rows_per_mod
    mspec = lambda k: pl.BlockSpec((1, 1, D_MODEL), lambda i, j, k=k: (mod_idx(i), 0, k))
    assert tr % FFN_SUB == 0 and FFN_SUB % cols == 0
    act = pl.pallas_call(
        functools.partial(_ffn_up_kernel, rows=rows, cols=cols),
        grid=(n // tr, nj),
        in_specs=[pl.BlockSpec((tr, D_MODEL), lambda i, j: (i, 0)),
                  mspec(3), mspec(4),
                  pl.BlockSpec((1, D_MODEL), lambda i, j: (0, 0)),
                  pl.BlockSpec((D_MODEL, 2 * tf), lambda i, j: (0, j)),
                  pl.BlockSpec((3, 3, tf), lambda i, j: (0, 0, j)),
                  pl.BlockSpec((1, tf), lambda i, j: (0, j))],
        out_specs=pl.BlockSpec((tr, tf), lambda i, j: (i, j)),
        out_shape=jax.ShapeDtypeStruct((n, D_FF), BF16),
        scratch_shapes=[pltpu.VMEM((tr, D_MODEL), BF16)],
        compiler_params=_cparams(("parallel", "arbitrary")),
        name="ffn_up",
    )(x, mod, mod, gain, w_ab, cw, cb)

    tm = min(rows_per_mod, 512)
    mod_idx2 = lambda i: (i * tm) // rows_per_mod
    final_norm = norm_final is not None
    in_specs = [pl.BlockSpec((tm, D_FF), lambda i: (i, 0)),
                pl.BlockSpec((tm, D_MODEL), lambda i: (i, 0)),
                pl.BlockSpec((1, 1, D_MODEL), lambda i: (mod_idx2(i), 0, 5)),
                pl.BlockSpec((D_FF, D_MODEL), lambda i: (0, 0))]
    args = [act, x, mod, w_out]
    if final_norm:
        in_specs.append(pl.BlockSpec((1, D_MODEL), lambda i: (0, 0)))
        args.append(norm_final)
    return pl.pallas_call(
        functools.partial(_ffn_down_kernel, final_norm=final_norm),
        grid=(n // tm,),
        in_specs=in_specs,
        out_specs=pl.BlockSpec((tm, D_MODEL), lambda i: (i, 0)),
        out_shape=jax.ShapeDtypeStruct((n, D_MODEL), F32),
        compiler_params=_cparams(("parallel",)),
        name="ffn_down",
    )(*args)


def _run_pass(x, mod_of_layer, rows_per_mod, n_seq, seq, conv_rows, conv_cols, s_h0, s_g0, want_states,
              prm, tables):
    n = n_seq * seq
    assert conv_rows & (conv_rows - 1) == 0 and conv_cols & (conv_cols - 1) == 0
    ffn_rows = min(rows_per_mod, max(seq, 2048))
    assert ffn_rows % seq == 0 and n % ffn_rows == 0
    state_shape = (n_seq, DEPTH, N_DIR, N_HEADS, HEAD_DIM, HEAD_DIM)
    states = (jnp.zeros(state_shape, F32), jnp.zeros(state_shape, F32)) if want_states else None
    for l in range(DEPTH):
        mod = mod_of_layer(l)
        z = _inproj_call(x, mod, prm["norm_mix"][l], prm["w_in"][l], prm["w_in_tail"][l], prm["gdn_conv"][l],
                         rows_per_mod, seq)
        oh, og, states = _mixer_call(z, prm["lb_logs"][l], prm["gdn_par"][l], tables, s_h0, s_g0, l,
                                     n_seq, seq, states)
        x = _post_call(oh, og, z, x, mod, prm["hgrn_norm"][l], prm["gdn_norm"][l], prm["w_proj_h"][l],
                       prm["w_proj_g"][l], prm["w_out"][l], rows_per_mod, tm=256)
        x = _ffn_call(x, mod, prm["norm_ffn"][l], prm["w_ffn_in"][l], prm["ffn_conv"][l], prm["ffn_conv_b"][l],
                      prm["w_ffn_out"][l], rows_per_mod, ffn_rows, conv_rows, conv_cols,
                      prm["norm_final"] if l == DEPTH - 1 else None)
    return x, states


def kernel(x_prompt, x_sample, c, state_hgrn, state_gdn, c_ctx, w_ada, b_ada, norm_mix, norm_ffn, w_in, hgrn_lb, hgrn_norm, gdn_conv, gdn_a_log, gdn_dt_bias, gdn_norm, w_proj_h, w_proj_g, w_out, w_ffn_in, ffn_conv, ffn_conv_b, w_ffn_out, norm_final):
    ctx_b, ctx_len, _ = x_prompt.shape
    dec_b, dec_len, _ = x_sample.shape

    lb_cum = jnp.cumsum(jax.nn.softmax(hgrn_lb.astype(F32), axis=0), axis=0)
    lower = lb_cum - lb_cum[0]
    lb_logs = jnp.stack([jnp.log(lower), jnp.log1p(-lower)], axis=2)
    small = 2 * N_DIR * N_HEADS
    tail_at = COL_GATE_H * D_MODEL
    w_in_tail = jnp.concatenate(
        [w_in[..., tail_at + small:], w_in[..., tail_at:tail_at + small],
         jnp.zeros((DEPTH, D_MODEL, TAIL_W - small), w_in.dtype)], axis=-1).astype(BF16)
    nj = D_FF // FFN_TILE
    w_ffn_ab = jnp.concatenate([w_ffn_in[..., :D_FF].reshape(DEPTH, D_MODEL, nj, FFN_TILE),
                                w_ffn_in[..., D_FF:].reshape(DEPTH, D_MODEL, nj, FFN_TILE)],
                               axis=-1).reshape(DEPTH, D_MODEL, 2 * D_FF).astype(BF16)
    gdn_par = jnp.zeros((DEPTH, 2, 128), F32)
    gdn_par = gdn_par.at[:, 0, TAIL_DECAY:TAIL_DECAY + N_DIR * N_HEADS].set(
        -jnp.exp(gdn_a_log.astype(F32)).reshape(DEPTH, -1))
    gdn_par = gdn_par.at[:, 1, TAIL_DECAY:TAIL_DECAY + N_DIR * N_HEADS].set(
        gdn_dt_bias.astype(F32).reshape(DEPTH, -1))
    prm = dict(
        norm_mix=norm_mix.reshape(DEPTH, 1, D_MODEL), norm_ffn=norm_ffn.reshape(DEPTH, 1, D_MODEL),
        w_in=_w_in_main(w_in), w_in_tail=w_in_tail, lb_logs=lb_logs, gdn_conv=gdn_conv, gdn_par=gdn_par,
        hgrn_norm=jnp.tile(hgrn_norm, (1, N_HEADS)).reshape(DEPTH, 1, D_MODEL),
        gdn_norm=jnp.tile(gdn_norm, (1, N_HEADS)).reshape(DEPTH, 1, D_MODEL),
        w_proj_h=w_proj_h.astype(BF16), w_proj_g=w_proj_g.astype(BF16), w_out=w_out.astype(BF16),
        w_ffn_in=w_ffn_ab, ffn_conv=ffn_conv, ffn_conv_b=ffn_conv_b.reshape(DEPTH, 1, D_FF),
        w_ffn_out=w_ffn_out.astype(BF16), norm_final=norm_final.reshape(1, D_MODEL))
    tables = _scan_tables()

    cond = jnp.concatenate([c_ctx[None], c, jnp.zeros((16 - 1 - dec_b, D_MODEL), F32)], axis=0)
    mod = _ada_call(cond, w_ada, b_ada)

    n_ctx = ctx_b * ctx_len
    yc, (new_h, new_g) = _run_pass(x_prompt.reshape(n_ctx, D_MODEL), lambda l: mod[l, 0:1, None, :], n_ctx,
                           ctx_b, ctx_len, 1, ctx_len, None, None, True, prm, tables)
    n_dec = dec_b * dec_len
    yd, _ = _run_pass(x_sample.reshape(n_dec, D_MODEL), lambda l: mod[l, 1:1 + dec_b, None, :], dec_len,
                         dec_b, dec_len, dec_len // GRID_W, GRID_W, state_hgrn, state_gdn, False, prm, tables)
    return (yc.reshape(ctx_b, ctx_len, D_MODEL), yd.reshape(dec_b, dec_len, D_MODEL),
            new_h, new_g)
```
